```python
import math
import jax, jax.numpy as jnp
from jax import lax
import numpy as np

D_MODEL = 1024
BATCH = 8
SEQ = 4096
DEPTH = 4

A_CHUNK = 128
A_GROUPS = 4
A_WIDTH = D_MODEL
A_GROUP_DIM = A_WIDTH // A_GROUPS
HEAD_DIM = 64
B_HEADS = 8
DILATED_PATTERNS = ((128, 1), (512, 4), (2048, 16))
N_DIL = len(DILATED_PATTERNS)
B_GROUP_WIDTH = B_HEADS * HEAD_DIM
B_QKV_WIDTH = N_DIL * B_GROUP_WIDTH
B_BLOCK = 128
ROPE_DIM = HEAD_DIM // 4
ROPE_THETA = 500000.0
N_BRANCH = 2
IN_WIDTH = 2 * A_WIDTH + 3 * B_QKV_WIDTH + N_BRANCH * D_MODEL
N_EXPERTS = 32
TOP_K = 4
D_FF_EXPERT = D_MODEL
SWIGLU_LIMIT = 7.0
SWIGLU_ALPHA = 1.702
DN_ALPHA = (2 * DEPTH) ** 0.25
DN_BETA = (8 * DEPTH) ** -0.25
LN_EPS = 1e-5
ADA_SCALE = 0.1

kernel_name = "hybrid_gmlp_dilated_attn_moe_deepnorm"


def layer_norm(x, g, b):
    xf = x.astype(jnp.float32)
    mu = jnp.mean(xf, -1, keepdims=True)
    var = jnp.mean(jnp.square(xf - mu), -1, keepdims=True)
    y = (xf - mu) * lax.rsqrt(var + LN_EPS)
    return (y * g.astype(jnp.float32) + b.astype(jnp.float32)).astype(x.dtype)


def plain_norm(x):
    xf = x.astype(jnp.float32)
    mu = jnp.mean(xf, -1, keepdims=True)
    var = jnp.mean(jnp.square(xf - mu), -1, keepdims=True)
    return ((xf - mu) * lax.rsqrt(var + LN_EPS)).astype(x.dtype)


def rope_partial(x, pos):
    half = ROPE_DIM // 2
    inv = ROPE_THETA ** (-jnp.arange(half, dtype=jnp.float32) / half)
    ang = pos.astype(jnp.float32)[:, None] * inv[None, :]
    cos = jnp.cos(ang)[None, :, None, :]
    sin = jnp.sin(ang)[None, :, None, :]
    xr = x[..., :ROPE_DIM].astype(jnp.float32)
    x1, x2 = xr[..., :half], xr[..., half:]
    rot = jnp.concatenate([x1 * cos - x2 * sin, x2 * cos + x1 * sin], -1)
    return jnp.concatenate([rot.astype(x.dtype), x[..., ROPE_DIM:]], -1)


def dilated_window_group(q, k, v, window, dilation):
    B, S, H, Dh = q.shape
    span = window // dilation
    L = S // dilation
    nb = -(-L // B_BLOCK)
    Lp = nb * B_BLOCK

    def sub(a):
        a = a.reshape(B, L, dilation, H, Dh)
        a = jnp.pad(a, ((0, 0), (0, Lp - L), (0, 0), (0, 0), (0, 0)))
        return a.reshape(B, nb, B_BLOCK, dilation, H, Dh)

    def with_prev(a):
        prev = jnp.pad(a[:, :-1], ((0, 0), (1, 0), (0, 0), (0, 0), (0, 0), (0, 0)))
        return jnp.concatenate([prev, a], axis=2)

    qb = sub(q)
    kc = with_prev(sub(k))
    vc = with_prev(sub(v))
    s = jnp.einsum('bnqrhc,bnkrhc->bnrhqk', qb, kc,
                   preferred_element_type=jnp.float32) * (HEAD_DIM ** -0.5)
    i = jnp.arange(B_BLOCK)[:, None]
    j = jnp.arange(2 * B_BLOCK)[None, :]
    diff = B_BLOCK + i - j
    band = (diff >= 0) & (diff <= span)
    kblk = jnp.arange(nb)[:, None] - 1 + (jnp.arange(2 * B_BLOCK)[None, :] >= B_BLOCK)
    valid = (kblk >= 0)[:, None, :]
    mask = band[None] & valid
    s = jnp.where(mask[None, :, None, None], s, -jnp.inf)
    m = jnp.max(s, -1)
    p = jnp.exp(s - m[..., None])
    den = jnp.sum(p, -1)
    o = jnp.einsum('bnrhqk,bnkrhc->bnqrhc', p, vc.astype(jnp.float32))

    def unsub(a):
        a = a.reshape((B, Lp, dilation) + a.shape[4:])[:, :L]
        return a.reshape((B, S) + a.shape[3:])

    m = unsub(jnp.transpose(m, (0, 1, 4, 2, 3)))
    den = unsub(jnp.transpose(den, (0, 1, 4, 2, 3)))
    return unsub(o), m, den


def token_mixer(h, w_in, sp_w, sp_b, w_ba, w_bb, w_out):
    B, S, D = h.shape
    pos = jnp.arange(S)
    z = h @ w_in
    i1 = A_WIDTH
    i2 = 2 * A_WIDTH
    i3 = i2 + B_QKV_WIDTH
    i4 = i3 + B_QKV_WIDTH
    i5 = i4 + B_QKV_WIDTH
    u, v, q, k, vv, gates = jnp.split(z, [i1, i2, i3, i4, i5], axis=-1)

    nc = S // A_CHUNK
    u = jax.nn.gelu(u).reshape(B, nc, A_CHUNK, A_GROUPS, A_GROUP_DIM)
    v = plain_norm(jax.nn.gelu(v).reshape(B, nc, A_CHUNK, A_GROUPS, A_GROUP_DIM))
    causal = jnp.tril(jnp.ones((A_CHUNK, A_CHUNK), dtype=bool))
    ws = jnp.where(causal[None], sp_w, jnp.zeros_like(sp_w))
    sv = jnp.einsum('gij,bnjgc->bnigc', ws, v) + sp_b.T[None, None, :, :, None]
    ya = (u * sv).reshape(B, S, A_WIDTH) @ w_ba

    q = rope_partial(q.reshape(B, S, N_DIL * B_HEADS, HEAD_DIM), pos).reshape(B, S, N_DIL, B_HEADS, HEAD_DIM)
    k = rope_partial(k.reshape(B, S, N_DIL * B_HEADS, HEAD_DIM), pos).reshape(B, S, N_DIL, B_HEADS, HEAD_DIM)
    vv = vv.reshape(B, S, N_DIL, B_HEADS, HEAD_DIM)
    outs, maxes, dens = [], [], []
    for g, (window, dilation) in enumerate(DILATED_PATTERNS):
        o_g, m_g, d_g = dilated_window_group(q[:, :, g], k[:, :, g], vv[:, :, g], window, dilation)
        outs.append(o_g)
        maxes.append(m_g)
        dens.append(d_g)
    m_all = jnp.stack(maxes, 0)
    m_ref = jnp.max(m_all, 0)
    wgt = jnp.exp(m_all - m_ref[None])
    num = sum(wgt[g][..., None] * outs[g] for g in range(N_DIL))
    den = jnp.sum(wgt * jnp.stack(dens, 0), 0)
    ob = (num / den[..., None]).astype(h.dtype).reshape(B, S, B_GROUP_WIDTH)
    yb = ob @ w_bb

    gts = jax.nn.sigmoid(gates.astype(jnp.float32)).astype(h.dtype).reshape(B, S, N_BRANCH, D)
    merged = gts[:, :, 0] * ya + gts[:, :, 1] * yb
    return merged @ w_out


def routed_experts(h, w_r, b_r, w_up, b_up, w_down, b_down):
    B, S, D = h.shape
    t = h.reshape(B * S, D)
    logits = (t @ w_r + b_r).astype(jnp.float32)
    vals, idx = lax.top_k(logits, TOP_K)
    wts = jax.nn.softmax(vals, axis=-1)
    combine = jnp.einsum('tk,tke->te', wts, jax.nn.one_hot(idx, N_EXPERTS, dtype=jnp.float32)).astype(t.dtype)
    out = jnp.zeros_like(t)
    for e in range(N_EXPERTS):
        hh = t @ w_up[e] + b_up[e]
        glu = jnp.minimum(hh[:, ::2], SWIGLU_LIMIT)
        lin = jnp.clip(hh[:, 1::2], -SWIGLU_LIMIT, SWIGLU_LIMIT)
        act = glu * jax.nn.sigmoid(SWIGLU_ALPHA * glu) * (lin + 1)
        out = out + combine[:, e:e + 1] * (act @ w_down[e] + b_down[e])
    return out.reshape(B, S, D)


def setup_inputs(seed: int = 0) -> dict:
    key = jax.random.key(seed)
    ks = jax.random.split(key, 22)
    n = jax.random.normal
    D = D_MODEL
    F = D_FF_EXPERT
    return {
        "x": n(ks[0], (BATCH, SEQ, D), jnp.float32),
        "c": n(ks[1], (BATCH, D), jnp.float32),
        "w_ada": n(ks[2], (DEPTH, D, 6 * D), jnp.float32) * (D ** -0.5) * ADA_SCALE,
        "b_ada": 0.01 * n(ks[3], (DEPTH, 6 * D), jnp.float32),
        "w_in": n(ks[4], (DEPTH, D, IN_WIDTH), jnp.float32) * (D ** -0.5),
        "spatial_w": n(ks[5], (DEPTH, A_GROUPS, A_CHUNK, A_CHUNK), jnp.float32) * (A_CHUNK ** -0.5),
        "spatial_b": 1.0 + 0.02 * n(ks[6], (DEPTH, A_GROUPS, A_CHUNK), jnp.float32),
        "w_branch_a": n(ks[7], (DEPTH, A_WIDTH, D), jnp.float32) * (A_WIDTH ** -0.5),
        "w_branch_b": n(ks[8], (DEPTH, B_GROUP_WIDTH, D), jnp.float32) * (B_GROUP_WIDTH ** -0.5),
        "w_out": n(ks[9], (DEPTH, D, D), jnp.float32) * (D ** -0.5) * DN_BETA,
        "ln1_g": 1.0 + 0.02 * n(ks[10], (DEPTH, D), jnp.float32),
        "ln1_b": 0.02 * n(ks[11], (DEPTH, D), jnp.float32),
        "w_router": n(ks[12], (DEPTH, D, N_EXPERTS), jnp.float32) * (D ** -0.5),
        "b_router": 0.01 * n(ks[13], (DEPTH, N_EXPERTS), jnp.float32),
        "w_up": n(ks[14], (DEPTH, N_EXPERTS, D, 2 * F), jnp.float32) * (D ** -0.5),
        "b_up": 0.01 * n(ks[15], (DEPTH, N_EXPERTS, 2 * F), jnp.float32),
        "w_down": n(ks[16], (DEPTH, N_EXPERTS, F, D), jnp.float32) * (F ** -0.5) * DN_BETA,
        "b_down": 0.01 * n(ks[17], (DEPTH, N_EXPERTS, D), jnp.float32),
        "ln2_g": 1.0 + 0.02 * n(ks[18], (DEPTH, D), jnp.float32),
        "ln2_b": 0.02 * n(ks[19], (DEPTH, D), jnp.float32),
    }


def reference(x, c, w_ada, b_ada, w_in, spatial_w, spatial_b, w_branch_a, w_branch_b, w_out,
              ln1_g, ln1_b, w_router, b_router, w_up, b_up, w_down, b_down, ln2_g, ln2_b):
    for l in range(DEPTH):
        mod = (c @ w_ada[l] + b_ada[l])[:, None, :]
        sh1, sc1, g1, sh2, sc2, g2 = jnp.split(mod, 6, axis=-1)
        h = x * (1 + sc1) + sh1
        y = token_mixer(h, w_in[l], spatial_w[l], spatial_b[l], w_branch_a[l], w_branch_b[l], w_out[l])
        x = layer_norm(DN_ALPHA * x + (1 + g1) * y, ln1_g[l], ln1_b[l])
        h = x * (1 + sc2) + sh2
        y = routed_experts(h, w_router[l], b_router[l], w_up[l], b_up[l], w_down[l], b_down[l])
        x = layer_norm(DN_ALPHA * x + (1 + g2) * y, ln2_g[l], ln2_b[l])
    return x
```

```python
import functools
import math

import jax
import jax.numpy as jnp
from jax import lax
from jax.experimental import pallas as pl
from jax.experimental.pallas import tpu as pltpu

BF16 = jnp.bfloat16
F32 = jnp.float32
U32 = jnp.uint32
I32 = jnp.int32

A_CHUNK = 128
A_GROUPS = 4
HEAD_DIM = 64
B_HEADS = 8
DILATIONS = (1, 4, 16)
ATT_SPAN = 128
B_GROUP_WIDTH = B_HEADS * HEAD_DIM
ROPE_DIM = HEAD_DIM // 4
ROPE_THETA = 500000.0
N_EXPERTS = 32
TOP_K = 4
SWIGLU_LIMIT = 7.0
SWIGLU_ALPHA = 1.702
LN_EPS = 1e-5

LANES = 128
VMEM_LIMIT = 56 * 1024 * 1024
TOKEN_TILE = 512
ATT_BLOCK = 128 * DILATIONS[-1]
EXPERT_TILE = 512


def _gelu(x):
    return 0.5 * x * (1.0 + jnp.tanh(math.sqrt(2.0 / math.pi) * (x + 0.044715 * (x * x * x))))


def _sigmoid(x):
    return 1.0 / (1.0 + jnp.exp(-x))


def _layer_norm(r, g, b):
    mu = jnp.mean(r, axis=-1, keepdims=True)
    c = r - mu
    var = jnp.mean(c * c, axis=-1, keepdims=True)
    return c * lax.rsqrt(var + LN_EPS) * g + b


def _pack_bf16_pairs(x):
    w = x.shape[1] // 2
    bits = lax.bitcast_convert_type(x.astype(BF16).astype(F32), U32)
    return bits[:, :w] | lax.shift_right_logical(bits[:, w:], jnp.uint32(16))


def _unpack_hi(wd):
    return lax.bitcast_convert_type(wd & jnp.uint32(0xFFFF0000), F32)


def _unpack_lo(wd):
    return lax.bitcast_convert_type(lax.shift_left(wd, jnp.uint32(16)), F32)


def _params(sem):
    return pltpu.CompilerParams(dimension_semantics=sem, vmem_limit_bytes=VMEM_LIMIT)


def _ada_body(c_ref, w_ref, b_ref, o_ref):
    o_ref[0] = jnp.dot(c_ref[...], w_ref[0], preferred_element_type=F32,
                       precision=lax.Precision.HIGHEST) + b_ref[0]


def _ada_mod(c, w_ada, b_ada):
    L, D, N = w_ada.shape
    B = c.shape[0]
    tn = D
    return pl.pallas_call(
        _ada_body,
        grid=(L, N // tn),
        in_specs=[
            pl.BlockSpec((B, D), lambda l, j: (0, 0)),
            pl.BlockSpec((1, D, tn), lambda l, j: (l, 0, j)),
            pl.BlockSpec((1, 1, tn), lambda l, j: (l, 0, j)),
        ],
        out_specs=pl.BlockSpec((1, B, tn), lambda l, j: (l, 0, j)),
        out_shape=jax.ShapeDtypeStruct((L, B, N), F32),
        compiler_params=_params(("arbitrary", "arbitrary")),
        name="ada_mod",
    )(c, w_ada, b_ada.reshape(L, 1, N))


def _branch_a_body(x_ref, sc_ref, sh_ref, wa_ref, spw_ref, spb_ref, wba_ref, o_ref, a_scr):
    tm, D = x_ref.shape[1], x_ref.shape[2]
    gd = D // A_GROUPS
    h = (x_ref[0] * (1.0 + sc_ref[0]) + sh_ref[0]).astype(BF16)
    row = lax.broadcasted_iota(I32, (A_CHUNK, A_CHUNK), 0)
    col = lax.broadcasted_iota(I32, (A_CHUNK, A_CHUNK), 1)
    causal = row >= col
    for g in range(A_GROUPS):
        v = _gelu(jnp.dot(h, wa_ref[0, :, D + g * gd:D + (g + 1) * gd], preferred_element_type=F32))
        mu = jnp.mean(v, axis=-1, keepdims=True)
        vc = v - mu
        var = jnp.mean(vc * vc, axis=-1, keepdims=True)
        vn = (vc * lax.rsqrt(var + LN_EPS)).astype(BF16)
        u = _gelu(jnp.dot(h, wa_ref[0, :, g * gd:(g + 1) * gd], preferred_element_type=F32))
        ws = jnp.where(causal, spw_ref[0, g], 0.0).astype(BF16)
        bias = spb_ref[0, g]
        for c in range(tm // A_CHUNK):
            rows = slice(c * A_CHUNK, (c + 1) * A_CHUNK)
            sv = jnp.dot(ws, vn[rows], preferred_element_type=F32) + bias
            a_scr[rows, g * gd:(g + 1) * gd] = (u[rows] * sv).astype(BF16)
    ya = jnp.dot(a_scr[...], wba_ref[0], preferred_element_type=F32)
    gate = _sigmoid(jnp.dot(h, wa_ref[0, :, 2 * D:3 * D], preferred_element_type=F32))
    o_ref[0] = (gate * ya).astype(BF16)


def _branch_a(x, sc, sh, w_a, sp_w, sp_b, w_ba, layer):
    B, S, D = x.shape
    tm = TOKEN_TILE
    return pl.pallas_call(
        _branch_a_body,
        grid=(B, S // tm),
        in_specs=[
            pl.BlockSpec((1, tm, D), lambda b, i: (b, i, 0)),
            pl.BlockSpec((1, 1, D), lambda b, i: (b, 0, 0)),
            pl.BlockSpec((1, 1, D), lambda b, i: (b, 0, 0)),
            pl.BlockSpec((1, D, 3 * D), lambda b, i: (layer, 0, 0)),
            pl.BlockSpec((1, A_GROUPS, A_CHUNK, A_CHUNK), lambda b, i: (layer, 0, 0, 0)),
            pl.BlockSpec((1, A_GROUPS, A_CHUNK, 1), lambda b, i: (layer, 0, 0, 0)),
            pl.BlockSpec((1, D, D), lambda b, i: (layer, 0, 0)),
        ],
        out_specs=pl.BlockSpec((1, tm, D), lambda b, i: (b, i, 0)),
        out_shape=jax.ShapeDtypeStruct((B, S, D), BF16),
        scratch_shapes=[pltpu.VMEM((tm, D), BF16)],
        compiler_params=_params(("arbitrary", "arbitrary")),
        name="branch_a",
    )(x, sc, sh, w_a, sp_w, sp_b, w_ba)


def _rope(z, cos, s_up, s_dn):
    half = ROPE_DIM // 2
    parts = []
    for c in range(z.shape[1] // LANES):
        zc = z[:, c * LANES:(c + 1) * LANES]
        parts.append(zc * cos + pltpu.roll(zc, half, 1) * s_up + pltpu.roll(zc, LANES - half, 1) * s_dn)
    return jnp.concatenate(parts, axis=1)


def _qkv_body(x_ref, sc_ref, sh_ref, w_ref, cos_ref, sup_ref, sdn_ref, *rest):
    outs = rest[:9]
    gate_ref = rest[9]
    scr = rest[10]
    tm, D = x_ref.shape[1], x_ref.shape[2]
    gw = B_GROUP_WIDTH
    h = (x_ref[0] * (1.0 + sc_ref[0]) + sh_ref[0]).astype(BF16)
    cos, s_up, s_dn = cos_ref[...], sup_ref[...], sdn_ref[...]
    for g, d in enumerate(DILATIONS):
        for part in range(3):
            col = (3 * g + part) * gw
            z = jnp.dot(h, w_ref[0, :, col:col + gw], preferred_element_type=F32)
            if part < 2:
                z = _rope(z, cos, s_up, s_dn)
            if part == 0:
                z = z * (HEAD_DIM ** -0.5)
            o_ref = outs[3 * g + part]
            if d == 1:
                o_ref[0, 0] = z.astype(BF16)
            else:
                nc = gw // LANES
                for c in range(nc):
                    scr[c] = z[:, c * LANES:(c + 1) * LANES]
                for r in range(d):
                    o_ref[0, r] = jnp.concatenate(
                        [scr[c, pl.ds(r, tm // d, stride=d), :] for c in range(nc)], axis=1).astype(BF16)
    gate = _sigmoid(jnp.dot(h, w_ref[0, :, 9 * gw:9 * gw + D], preferred_element_type=F32))
    gate_ref[0] = gate.astype(BF16)


def _qkv(x, sc, sh, w_c, cos_t, sup_t, sdn_t, layer):
    B, S, D = x.shape
    tm = TOKEN_TILE
    gw = B_GROUP_WIDTH
    nw = w_c.shape[2]
    out_shapes, out_specs = [], []
    for d in DILATIONS:
        for _ in range(3):
            out_shapes.append(jax.ShapeDtypeStruct((B, d, S // d, gw), BF16))
            out_specs.append(pl.BlockSpec((1, d, tm // d, gw), lambda b, i: (b, 0, i, 0)))
    out_shapes.append(jax.ShapeDtypeStruct((B, S, D), BF16))
    out_specs.append(pl.BlockSpec((1, tm, D), lambda b, i: (b, i, 0)))
    return pl.pallas_call(
        _qkv_body,
        grid=(B, S // tm),
        in_specs=[
            pl.BlockSpec((1, tm, D), lambda b, i: (b, i, 0)),
            pl.BlockSpec((1, 1, D), lambda b, i: (b, 0, 0)),
            pl.BlockSpec((1, 1, D), lambda b, i: (b, 0, 0)),
            pl.BlockSpec((1, D, nw), lambda b, i: (layer, 0, 0), pipeline_mode=pl.Buffered(1)),
            pl.BlockSpec((tm, LANES), lambda b, i: (i, 0)),
            pl.BlockSpec((tm, LANES), lambda b, i: (i, 0)),
            pl.BlockSpec((tm, LANES), lambda b, i: (i, 0)),
        ],
        out_specs=out_specs,
        out_shape=out_shapes,
        scratch_shapes=[pltpu.VMEM((gw // LANES, tm, LANES), F32)],
        compiler_params=_params(("arbitrary", "arbitrary")),
        name="qkv",
    )(x, sc, sh, w_c, cos_t, sup_t, sdn_t)


def _attention_body(*refs):
    ins = refs[:15]
    o_ref = refs[15]
    o_scr, m_scr, l_scr = refs[16:19]
    n = pl.program_id(1)
    blk = ATT_SPAN
    qi = lax.broadcasted_iota(I32, (blk, 2 * blk), 0)
    kj = lax.broadcasted_iota(I32, (blk, 2 * blk), 1)
    band = (kj >= qi) & (kj <= qi + ATT_SPAN)
    neg = jnp.float32(-jnp.inf)
    bias_full = jnp.where(band, 0.0, neg)
    bias_cur = jnp.where(band & (kj >= blk), 0.0, neg)
    bias_first = jnp.where(n == 0, bias_cur, bias_full)
    lane = lax.broadcasted_iota(I32, (blk, LANES), 1)
    head0 = lane < HEAD_DIM

    def unit(g, d, r, j, q, kk, vv, bias):
        zero = jnp.zeros_like(q)
        res = []
        for qh in (jnp.where(head0, q, zero), jnp.where(head0, zero, q)):
            s = lax.dot_general(qh, kk, (((1,), (1,)), ((), ())), preferred_element_type=F32) + bias
            m = jnp.max(s, axis=1, keepdims=True)
            p = jnp.exp(s - m)
            l = jnp.sum(p, axis=1, keepdims=True)
            o = jnp.dot(p.astype(BF16), vv, preferred_element_type=F32)
            res.append((o, m, l))
        o_u = jnp.where(head0, res[0][0], res[1][0])
        m_u = jnp.where(head0, res[0][1], res[1][1])
        l_u = jnp.where(head0, res[0][2], res[1][2])
        base = pl.multiple_of(j * (blk * d), blk * d)
        for scr, val in ((o_scr, o_u), (m_scr, m_u), (l_scr, l_u)):
            view = scr.at[g, pl.ds(base, blk * d)]
            if d == 1:
                view[...] = val
            else:
                view[pl.ds(r, blk, stride=d), :] = val

    for g, d in enumerate(DILATIONS):
        q_ref, kc_ref, kp_ref, vc_ref, vp_ref = ins[5 * g:5 * g + 5]
        nj = ATT_BLOCK // d // blk
        for r in range(d):
            kk = jnp.concatenate([kp_ref[0, r], kc_ref[0, r, 0:blk]], axis=0)
            vv = jnp.concatenate([vp_ref[0, r], vc_ref[0, r, 0:blk]], axis=0)
            unit(g, d, r, 0, q_ref[0, r, 0:blk], kk, vv, bias_first)
            if nj > 1:
                def body(j, carry, g=g, d=d, r=r, q_ref=q_ref, kc_ref=kc_ref, vc_ref=vc_ref):
                    q0 = pl.multiple_of(j * blk, blk)
                    k0 = pl.multiple_of((j - 1) * blk, blk)
                    unit(g, d, r, j, q_ref[0, r, pl.ds(q0, blk)], kc_ref[0, r, pl.ds(k0, 2 * blk)],
                         vc_ref[0, r, pl.ds(k0, 2 * blk)], bias_full)
                    return carry
                lax.fori_loop(1, nj, body, 0)

    def merge(c, carry):
        rows = pl.ds(pl.multiple_of(c * blk, blk), blk)
        ms = [m_scr[g, rows] for g in range(3)]
        m_ref = jnp.maximum(jnp.maximum(ms[0], ms[1]), ms[2])
        num = jnp.zeros((blk, LANES), F32)
        den = jnp.zeros((blk, LANES), F32)
        for g in range(3):
            w = jnp.exp(ms[g] - m_ref)
            num = num + w * o_scr[g, rows]
            den = den + w * l_scr[g, rows]
        o_ref[0, rows] = (num / den).astype(BF16)
        return carry
    lax.fori_loop(0, ATT_BLOCK // blk, merge, 0)


def _attention(qkv):
    B = qkv[0].shape[0]
    S = qkv[0].shape[2]
    gw = B_GROUP_WIDTH
    blk = ATT_SPAN
    nblk = S // ATT_BLOCK
    in_specs, args = [], []
    for g, d in enumerate(DILATIONS):
        rows = ATT_BLOCK // d
        per = rows // blk
        cur = pl.BlockSpec((1, d, rows, LANES), lambda b, n, hp: (b, 0, n, hp))
        prev = pl.BlockSpec((1, d, blk, LANES),
                            lambda b, n, hp, per=per: (b, 0, jnp.maximum(n * per - 1, 0), hp))
        q, k, v = qkv[3 * g:3 * g + 3]
        in_specs += [cur, cur, prev, cur, prev]
        args += [q, k, k, v, v]
    return pl.pallas_call(
        _attention_body,
        grid=(B, nblk, gw // LANES),
        in_specs=in_specs,
        out_specs=pl.BlockSpec((1, ATT_BLOCK, LANES), lambda b, n, hp: (b, n, hp)),
        out_shape=jax.ShapeDtypeStruct((B, S, gw), BF16),
        scratch_shapes=[pltpu.VMEM((3, ATT_BLOCK, LANES), F32)] * 3,
        compiler_params=_params(("arbitrary", "arbitrary", "arbitrary")),
        name="attention",
    )(*args)


def _post_mixer_body(alpha, x_ref, ya_ref, gb_ref, ob_ref, wbb_ref, wout_ref, gm_ref, lng_ref, lnb_ref,
                     sc_ref, sh_ref, wr_ref, br_ref,
                     x1_ref, hp_ref, idx_ref, rank_ref, wts_ref, cnt_ref, carry):
    tt = x_ref.shape[1]
    first = (pl.program_id(0) == 0) & (pl.program_id(1) == 0)

    @pl.when(first)
    def _():
        carry[...] = jnp.zeros_like(carry)

    yb = jnp.dot(ob_ref[0], wbb_ref[0], preferred_element_type=F32)
    merged = (ya_ref[0].astype(F32) + gb_ref[0].astype(F32) * yb).astype(BF16)
    y = jnp.dot(merged, wout_ref[0], preferred_element_type=F32)
    x1 = _layer_norm(alpha * x_ref[0] + (1.0 + gm_ref[0]) * y, lng_ref[0], lnb_ref[0])
    x1_ref[0] = x1
    h2 = x1 * (1.0 + sc_ref[0]) + sh_ref[0]
    hp_ref[...] = _pack_bf16_pairs(h2)

    logits = lax.dot_general(wr_ref[0], h2, (((1,), (1,)), ((), ())), preferred_element_type=F32,
                             precision=lax.Precision.HIGHEST) + br_ref[0]
    eid = lax.broadcasted_iota(I32, (N_EXPERTS, tt), 0)
    vals, hots = [], []
    cur = logits
    for k in range(TOP_K):
        m = jnp.max(cur, axis=0, keepdims=True)
        sel = jnp.min(jnp.where(cur == m, eid, N_EXPERTS), axis=0, keepdims=True)
        hot = eid == sel
        idx_ref[k:k + 1, :] = sel
        vals.append(m)
        hots.append(hot)
        cur = jnp.where(hot, -jnp.inf, cur)
    ex = [jnp.exp(v - vals[0]) for v in vals]
    tot = ex[0] + ex[1] + ex[2] + ex[3]
    for k in range(TOP_K):
        wts_ref[k:k + 1, :] = ex[k] / tot

    member = jnp.zeros((N_EXPERTS, tt), F32)
    for hot in hots:
        member = member + hot.astype(F32)
    before = (lax.broadcasted_iota(I32, (tt, tt), 0) < lax.broadcasted_iota(I32, (tt, tt), 1))
    excl = jnp.dot(member.astype(BF16), before.astype(BF16), preferred_element_type=F32)
    base = carry[:, 0:1] + excl
    for k in range(TOP_K):
        rank_ref[k:k + 1, :] = jnp.sum(jnp.where(hots[k], base, 0.0), axis=0, keepdims=True).astype(I32)
    carry[...] = carry[...] + jnp.sum(member, axis=1, keepdims=True)
    cnt_ref[...] = carry[...]


def _post_mixer(x, ya, gb, ob, w_bb, w_out, gm, ln_g, ln_b, sc, sh, w_rt, b_r, layer, depth):
    B, S, D = x.shape
    tt = TOKEN_TILE
    nt = S // tt
    T = B * S
    gw = B_GROUP_WIDTH
    tok = lambda b, i: (b, i, 0)
    per_b = lambda b, i: (b, 0, 0)
    per_l3 = lambda b, i: (layer, 0, 0)
    flat = lambda b, i: (0, b * nt + i)
    return pl.pallas_call(
        functools.partial(_post_mixer_body, (2.0 * depth) ** 0.25),
        grid=(B, nt),
        in_specs=[
            pl.BlockSpec((1, tt, D), tok),
            pl.BlockSpec((1, tt, D), tok),
            pl.BlockSpec((1, tt, D), tok),
            pl.BlockSpec((1, tt, gw), tok),
            pl.BlockSpec((1, gw, D), per_l3),
            pl.BlockSpec((1, D, D), per_l3),
            pl.BlockSpec((1, 1, D), per_b),
            pl.BlockSpec((1, 1, D), per_l3),
            pl.BlockSpec((1, 1, D), per_l3),
            pl.BlockSpec((1, 1, D), per_b),
            pl.BlockSpec((1, 1, D), per_b),
            pl.BlockSpec((1, N_EXPERTS, D), per_l3),
            pl.BlockSpec((1, N_EXPERTS, 1), per_l3),
        ],
        out_specs=[
            pl.BlockSpec((1, tt, D), tok),
            pl.BlockSpec((tt, D // 2), lambda b, i: (b * nt + i, 0)),
            pl.BlockSpec((TOP_K, tt), flat),
            pl.BlockSpec((TOP_K, tt), flat),
            pl.BlockSpec((TOP_K, tt), flat),
            pl.BlockSpec((N_EXPERTS, LANES), lambda b, i: (0, 0)),
        ],
        out_shape=[
            jax.ShapeDtypeStruct((B, S, D), F32),
            jax.ShapeDtypeStruct((T, D // 2), U32),
            jax.ShapeDtypeStruct((TOP_K, T), I32),
            jax.ShapeDtypeStruct((TOP_K, T), I32),
            jax.ShapeDtypeStruct((TOP_K, T), F32),
            jax.ShapeDtypeStruct((N_EXPERTS, LANES), F32),
        ],
        scratch_shapes=[pltpu.VMEM((N_EXPERTS, LANES), F32)],
        compiler_params=_params(("arbitrary", "arbitrary")),
        name="post_mixer",
    )(x, ya, gb, ob, w_bb, w_out, gm, ln_g, ln_b, sc, sh, w_rt, b_r)


def _dispatch_body(ends_ref, padded_ref, h_ref, slot_ref, xs_ref, zbuf, sem):
    tt = h_ref.shape[0]
    tm = zbuf.shape[0]

    @pl.when(pl.program_id(0) == 0)
    def _():
        zbuf[...] = jnp.zeros_like(zbuf)
        for e in range(N_EXPERTS):
            @pl.when(padded_ref[e] > 0)
            def _():
                start = pl.multiple_of(ends_ref[e] - tm, tm)
                cp = pltpu.make_async_copy(zbuf, xs_ref.at[pl.ds(start, tm)], sem)
                cp.start()
                cp.wait()

        def fill_unused(t, carry):
            cp = pltpu.make_async_copy(zbuf, xs_ref.at[pl.ds(pl.multiple_of(t * tm, tm), tm)], sem)
            cp.start()
            cp.wait()
            return carry
        lax.fori_loop(ends_ref[N_EXPERTS - 1] // tm, xs_ref.shape[0] // tm, fill_unused, 0)

    def issue(j, carry):
        for k in range(TOP_K):
            pltpu.make_async_copy(h_ref.at[pl.ds(j, 1)], xs_ref.at[pl.ds(slot_ref[k, j], 1)], sem).start()
        return carry
    lax.fori_loop(0, tt, issue, 0)
    for k in range(TOP_K):
        pltpu.make_async_copy(h_ref, xs_ref.at[pl.ds(0, tt)], sem).wait()


def _dispatch(hp, slot, ends, padded, n_slots):
    T, W = hp.shape
    tt = TOKEN_TILE
    grid_spec = pltpu.PrefetchScalarGridSpec(
        num_scalar_prefetch=2,
        grid=(T // tt,),
        in_specs=[
            pl.BlockSpec((tt, W), lambda i, *_: (i, 0)),
            pl.BlockSpec((TOP_K, tt), lambda i, *_: (0, i), memory_space=pltpu.SMEM),
        ],
        out_specs=pl.BlockSpec(memory_space=pl.ANY),
        scratch_shapes=[pltpu.VMEM((EXPERT_TILE, W), U32), pltpu.SemaphoreType.DMA],
    )
    return pl.pallas_call(
        _dispatch_body,
        grid_spec=grid_spec,
        out_shape=jax.ShapeDtypeStruct((n_slots, W), U32),
        compiler_params=pltpu.CompilerParams(dimension_semantics=("arbitrary",), vmem_limit_bytes=VMEM_LIMIT,
                                             has_side_effects=True),
        name="dispatch",
    )(ends, padded, hp, slot)


def _experts_body(te_ref, nact_ref, xs_ref, wup_ref, bup_ref, wdn_ref, bdn_ref, ys_ref):
    i = pl.program_id(0)
    ff = wdn_ref.shape[1]

    @pl.when(i < nact_ref[0])
    def _():
        wd = xs_ref[...]
        x = jnp.concatenate([_unpack_hi(wd), _unpack_lo(wd)], axis=1).astype(BF16)
        hh = jnp.dot(x, wup_ref[0], preferred_element_type=F32) + bup_ref[0]
        glu = jnp.minimum(hh[:, :ff], SWIGLU_LIMIT)
        lin = jnp.clip(hh[:, ff:], -SWIGLU_LIMIT, SWIGLU_LIMIT)
        act = glu * _sigmoid(SWIGLU_ALPHA * glu) * (lin + 1.0)
        y = jnp.dot(act.astype(BF16), wdn_ref[0], preferred_element_type=F32) + bdn_ref[0]
        ys_ref[...] = _pack_bf16_pairs(y)

    @pl.when(i >= nact_ref[0])
    def _():
        ys_ref[...] = jnp.zeros_like(ys_ref)


def _experts(xs, tile_expert, n_active, w_up, b_up, w_dn, b_dn, layer):
    P, W = xs.shape
    tm = EXPERT_TILE
    D = 2 * W
    ff2 = w_up.shape[2]
    ff = w_dn.shape[1]
    ex = lambda i, te, na: (layer * N_EXPERTS + te[i], 0, 0)
    grid_spec = pltpu.PrefetchScalarGridSpec(
        num_scalar_prefetch=2,
        grid=(P // tm,),
        in_specs=[
            pl.BlockSpec((tm, W), lambda i, te, na: (jnp.where(i < na[0], i, 0), 0)),
            pl.BlockSpec((1, D, ff2), ex),
            pl.BlockSpec((1, 1, ff2), ex),
            pl.BlockSpec((1, ff, D), ex),
            pl.BlockSpec((1, 1, D), ex),
        ],
        out_specs=pl.BlockSpec((tm, W), lambda i, te, na: (i, 0)),
    )
    return pl.pallas_call(
        _experts_body,
        grid_spec=grid_spec,
        out_shape=jax.ShapeDtypeStruct((P, W), U32),
        compiler_params=_params(("arbitrary",)),
        name="experts",
    )(tile_expert, n_active, xs, w_up, b_up, w_dn, b_dn)


def _combine_body(alpha, x_ref, slot_ref, wts_ref, ys_ref, gm_ref, lng_ref, lnb_ref, o_ref, ybuf, sem):
    tt = x_ref.shape[1]

    def issue(j, carry):
        for k in range(TOP_K):
            pltpu.make_async_copy(ys_ref.at[pl.ds(slot_ref[k, j], 1)], ybuf.at[k, pl.ds(j, 1)], sem).start()
        return carry
    lax.fori_loop(0, tt, issue, 0)
    for k in range(TOP_K):
        pltpu.make_async_copy(ys_ref.at[pl.ds(0, tt)], ybuf.at[k], sem).wait()

    hi = jnp.zeros((tt, ybuf.shape[2]), F32)
    lo = jnp.zeros((tt, ybuf.shape[2]), F32)
    for k in range(TOP_K):
        w = wts_ref[:, k:k + 1]
        wd = ybuf[k]
        hi = hi + w * _unpack_hi(wd)
        lo = lo + w * _unpack_lo(wd)
    y = jnp.concatenate([hi, lo], axis=1)
    o_ref[0] = _layer_norm(alpha * x_ref[0] + (1.0 + gm_ref[0]) * y, lng_ref[0], lnb_ref[0])


def _combine(x1, slot, wts_t, ys, gm, ln_g, ln_b, layer, depth):
    B, S, D = x1.shape
    tt = TOKEN_TILE
    nt = S // tt
    return pl.pallas_call(
        functools.partial(_combine_body, (2.0 * depth) ** 0.25),
        grid=(B, nt),
        in_specs=[
            pl.BlockSpec((1, tt, D), lambda b, i: (b, i, 0)),
            pl.BlockSpec((TOP_K, tt), lambda b, i: (0, b * nt + i), memory_space=pltpu.SMEM),
            pl.BlockSpec((tt, TOP_K), lambda b, i: (b * nt + i, 0)),
            pl.BlockSpec(memory_space=pl.ANY),
            pl.BlockSpec((1, 1, D), lambda b, i: (b, 0, 0)),
            pl.BlockSpec((1, 1, D), lambda b, i: (layer, 0, 0)),
            pl.BlockSpec((1, 1, D), lambda b, i: (layer, 0, 0)),
        ],
        out_specs=pl.BlockSpec((1, tt, D), lambda b, i: (b, i, 0)),
        out_shape=jax.ShapeDtypeStruct((B, S, D), F32),
        scratch_shapes=[pltpu.VMEM((TOP_K, tt, D // 2), U32), pltpu.SemaphoreType.DMA],
        compiler_params=_params(("arbitrary", "arbitrary")),
        name="combine",
    )(x1, slot, wts_t, ys, gm, ln_g, ln_b)


def _rope_tables(S):
    half = ROPE_DIM // 2
    inv = ROPE_THETA ** (-jnp.arange(half, dtype=F32) / half)
    ang = jnp.arange(S, dtype=F32)[:, None] * inv[None, :]
    cos, sin = jnp.cos(ang), jnp.sin(ang)
    lane = jnp.arange(LANES) % HEAD_DIM
    pick = lane % half
    cos_l, sin_l = cos[:, pick], sin[:, pick]
    cos_t = jnp.where(lane < ROPE_DIM, cos_l, 1.0)
    s_up = jnp.where((lane >= half) & (lane < ROPE_DIM), sin_l, 0.0)
    s_dn = jnp.where(lane < half, -sin_l, 0.0)
    return cos_t.astype(F32), s_up.astype(F32), s_dn.astype(F32)


def kernel(x, c, w_ada, b_ada, w_in, spatial_w, spatial_b, w_branch_a, w_branch_b, w_out, ln1_g, ln1_b,
           w_router, b_router, w_up, b_up, w_down, b_down, ln2_g, ln2_b):
    B, S, D = x.shape
    L = w_ada.shape[0]
    T = B * S
    gw = B_GROUP_WIDTH
    nq = len(DILATIONS) * gw
    assert S % ATT_BLOCK == 0 and S % TOKEN_TILE == 0 and D % (2 * LANES) == 0

    q0, k0, v0, g0 = 2 * D, 2 * D + nq, 2 * D + 2 * nq, 2 * D + 3 * nq
    w_a = jnp.concatenate([w_in[:, :, :2 * D], w_in[:, :, g0:g0 + D]], axis=2).astype(BF16)
    cols = []
    for g in range(len(DILATIONS)):
        for base in (q0, k0, v0):
            cols.append(w_in[:, :, base + g * gw:base + (g + 1) * gw])
    cols.append(w_in[:, :, g0 + D:g0 + 2 * D])
    w_c = jnp.concatenate(cols, axis=2).astype(BF16)
    w_ba = w_branch_a.astype(BF16)
    w_bb = w_branch_b.astype(BF16)
    w_o = w_out.astype(BF16)
    sp_b = spatial_b[..., None]
    w_rt = jnp.swapaxes(w_router, 1, 2)
    b_r = b_router[..., None]
    ff = w_down.shape[2]
    w_up_p = jnp.concatenate([w_up[..., 0::2], w_up[..., 1::2]], axis=-1).astype(BF16).reshape(L * N_EXPERTS, D, -1)
    b_up_p = jnp.concatenate([b_up[..., 0::2], b_up[..., 1::2]], axis=-1).reshape(L * N_EXPERTS, 1, -1)
    w_dn_p = w_down.astype(BF16).reshape(L * N_EXPERTS, ff, D)
    b_dn_p = b_down.reshape(L * N_EXPERTS, 1, D)
    ln1g, ln1b = ln1_g[:, None, :], ln1_b[:, None, :]
    ln2g, ln2b = ln2_g[:, None, :], ln2_b[:, None, :]
    cos_t, sup_t, sdn_t = _rope_tables(S)

    mod = _ada_mod(c, w_ada, b_ada)
    n_slots = T * TOP_K + N_EXPERTS * EXPERT_TILE
    n_tiles = n_slots // EXPERT_TILE

    for l in range(L):
        sh1, sc1, g1, sh2, sc2, g2 = [mod[l, :, None, i * D:(i + 1) * D] for i in range(6)]
        ya = _branch_a(x, sc1, sh1, w_a, spatial_w, sp_b, w_ba, l)
        *qkv, gb = _qkv(x, sc1, sh1, w_c, cos_t, sup_t, sdn_t, l)
        ob = _attention(qkv)
        x1, hp, idx, rank, wts, cnt = _post_mixer(x, ya, gb, ob, w_bb, w_o, g1, ln1g, ln1b, sc2, sh2,
                                                  w_rt, b_r, l, L)
        counts = cnt[:, 0].astype(I32)
        padded = ((counts + EXPERT_TILE - 1) // EXPERT_TILE) * EXPERT_TILE
        ends = jnp.cumsum(padded)
        starts = ends - padded
        slot = rank + jnp.sum(jnp.where(idx[None] == jnp.arange(N_EXPERTS, dtype=I32)[:, None, None],
                                        starts[:, None, None], 0), axis=0)
        n_active = (ends[-1] // EXPERT_TILE).astype(I32)
        tile_start = jnp.arange(n_tiles, dtype=I32) * EXPERT_TILE
        te = jnp.sum((ends[None, :] <= tile_start[:, None]).astype(I32), axis=1)
        te_last = jnp.sum((ends <= (n_active - 1) * EXPERT_TILE).astype(I32))
        te = jnp.minimum(jnp.where(tile_start < ends[-1], te, te_last), N_EXPERTS - 1).astype(I32)

        xs = _dispatch(hp, slot, ends.astype(I32), padded.astype(I32), n_slots)
        ys = _experts(xs, te, n_active.reshape(1), w_up_p, b_up_p, w_dn_p, b_dn_p, l)
        x = _combine(x1, slot, wts.T, ys, g2, ln2g, ln2b, l, L)
    return x
```

```python
import functools
import math

import jax
import jax.numpy as jnp
from jax import lax
from jax.experimental import pallas as pl
from jax.experimental.pallas import tpu as pltpu

BF16 = jnp.bfloat16
F32 = jnp.float32
U32 = jnp.uint32
I32 = jnp.int32

A_CHUNK = 128
A_GROUPS = 4
HEAD_DIM = 64
B_HEADS = 8
DILATIONS = (1, 4, 16)
ATT_SPAN = 128
B_GROUP_WIDTH = B_HEADS * HEAD_DIM
ROPE_DIM = HEAD_DIM // 4
ROPE_THETA = 500000.0
N_EXPERTS = 32
TOP_K = 4
SWIGLU_LIMIT = 7.0
SWIGLU_ALPHA = 1.702
LN_EPS = 1e-5

LANES = 128
VMEM_LIMIT = 56 * 1024 * 1024
TOKEN_TILE = 512
ATT_BLOCK = 128 * DILATIONS[-1]
EXPERT_TILE = 512
ATT_CHUNK = 4


def _gelu(x):
    return 0.5 * x * (1.0 + jnp.tanh(math.sqrt(2.0 / math.pi) * (x + 0.044715 * (x * x * x))))


def _sigmoid(x):
    return 1.0 / (1.0 + jnp.exp(-x))


def _layer_norm(r, g, b):
    mu = jnp.mean(r, axis=-1, keepdims=True)
    c = r - mu
    var = jnp.mean(c * c, axis=-1, keepdims=True)
    return c * lax.rsqrt(var + LN_EPS) * g + b


def _pack_bf16_pairs(x):
    w = x.shape[1] // 2
    bits = lax.bitcast_convert_type(x.astype(BF16).astype(F32), U32)
    return bits[:, :w] | lax.shift_right_logical(bits[:, w:], jnp.uint32(16))


def _unpack_hi(wd):
    return lax.bitcast_convert_type(wd & jnp.uint32(0xFFFF0000), F32)


def _unpack_lo(wd):
    return lax.bitcast_convert_type(lax.shift_left(wd, jnp.uint32(16)), F32)


def _params(sem):
    return pltpu.CompilerParams(dimension_semantics=sem, vmem_limit_bytes=VMEM_LIMIT)


def _ada_body(c_ref, w_ref, b_ref, o_ref):
    o_ref[0] = jnp.dot(c_ref[...], w_ref[0], preferred_element_type=F32,
                       precision=lax.Precision.HIGHEST) + b_ref[0]


def _ada_mod(c, w_ada, b_ada):
    L, D, N = w_ada.shape
    B = c.shape[0]
    tn = D
    return pl.pallas_call(
        _ada_body,
        grid=(L, N // tn),
        in_specs=[
            pl.BlockSpec((B, D), lambda l, j: (0, 0)),
            pl.BlockSpec((1, D, tn), lambda l, j: (l, 0, j)),
            pl.BlockSpec((1, 1, tn), lambda l, j: (l, 0, j)),
        ],
        out_specs=pl.BlockSpec((1, B, tn), lambda l, j: (l, 0, j)),
        out_shape=jax.ShapeDtypeStruct((L, B, N), F32),
        compiler_params=_params(("arbitrary", "arbitrary")),
        name="ada_mod",
    )(c, w_ada, b_ada.reshape(L, 1, N))


def _branch_a_body(x_ref, sc_ref, sh_ref, wa_ref, spw_ref, spb_ref, wba_ref, o_ref, a_scr):
    tm, D = x_ref.shape[1], x_ref.shape[2]
    gd = D // A_GROUPS
    h = (x_ref[0] * (1.0 + sc_ref[0]) + sh_ref[0]).astype(BF16)
    row = lax.broadcasted_iota(I32, (A_CHUNK, A_CHUNK), 0)
    col = lax.broadcasted_iota(I32, (A_CHUNK, A_CHUNK), 1)
    causal = row >= col
    for g in range(A_GROUPS):
        v = _gelu(jnp.dot(h, wa_ref[0, :, D + g * gd:D + (g + 1) * gd], preferred_element_type=F32))
        mu = jnp.mean(v, axis=-1, keepdims=True)
        vc = v - mu
        var = jnp.mean(vc * vc, axis=-1, keepdims=True)
        vn = (vc * lax.rsqrt(var + LN_EPS)).astype(BF16)
        u = _gelu(jnp.dot(h, wa_ref[0, :, g * gd:(g + 1) * gd], preferred_element_type=F32))
        ws = jnp.where(causal, spw_ref[0, g], 0.0).astype(BF16)
        bias = spb_ref[0, g]
        for c in range(tm // A_CHUNK):
            rows = slice(c * A_CHUNK, (c + 1) * A_CHUNK)
            sv = jnp.dot(ws, vn[rows], preferred_element_type=F32) + bias
            a_scr[rows, g * gd:(g + 1) * gd] = (u[rows] * sv).astype(BF16)
    ya = jnp.dot(a_scr[...], wba_ref[0], preferred_element_type=F32)
    gate = _sigmoid(jnp.dot(h, wa_ref[0, :, 2 * D:3 * D], preferred_element_type=F32))
    o_ref[0] = (gate * ya).astype(BF16)


def _branch_a(x, sc, sh, w_a, sp_w, sp_b, w_ba, layer):
    B, S, D = x.shape
    tm = TOKEN_TILE
    return pl.pallas_call(
        _branch_a_body,
        grid=(B, S // tm),
        in_specs=[
            pl.BlockSpec((1, tm, D), lambda b, i: (b, i, 0)),
            pl.BlockSpec((1, 1, D), lambda b, i: (b, 0, 0)),
            pl.BlockSpec((1, 1, D), lambda b, i: (b, 0, 0)),
            pl.BlockSpec((1, D, 3 * D), lambda b, i: (layer, 0, 0)),
            pl.BlockSpec((1, A_GROUPS, A_CHUNK, A_CHUNK), lambda b, i: (layer, 0, 0, 0)),
            pl.BlockSpec((1, A_GROUPS, A_CHUNK, 1), lambda b, i: (layer, 0, 0, 0)),
            pl.BlockSpec((1, D, D), lambda b, i: (layer, 0, 0)),
        ],
        out_specs=pl.BlockSpec((1, tm, D), lambda b, i: (b, i, 0)),
        out_shape=jax.ShapeDtypeStruct((B, S, D), BF16),
        scratch_shapes=[pltpu.VMEM((tm, D), BF16)],
        compiler_params=_params(("arbitrary", "arbitrary")),
        name="branch_a",
    )(x, sc, sh, w_a, sp_w, sp_b, w_ba)


def _rope(z, cos, s_up, s_dn):
    half = ROPE_DIM // 2
    parts = []
    for c in range(z.shape[1] // LANES):
        zc = z[:, c * LANES:(c + 1) * LANES]
        parts.append(zc * cos + pltpu.roll(zc, half, 1) * s_up + pltpu.roll(zc, LANES - half, 1) * s_dn)
    return jnp.concatenate(parts, axis=1)


def _qkv_body(x_ref, sc_ref, sh_ref, w_ref, cos_ref, sup_ref, sdn_ref, *rest):
    outs = rest[:9]
    gate_ref = rest[9]
    scr = rest[10]
    tm, D = x_ref.shape[1], x_ref.shape[2]
    gw = B_GROUP_WIDTH
    h = (x_ref[0] * (1.0 + sc_ref[0]) + sh_ref[0]).astype(BF16)
    cos, s_up, s_dn = cos_ref[...], sup_ref[...], sdn_ref[...]
    for g, d in enumerate(DILATIONS):
        for part in range(3):
            col = (3 * g + part) * gw
            z = jnp.dot(h, w_ref[0, :, col:col + gw], preferred_element_type=F32)
            if part < 2:
                z = _rope(z, cos, s_up, s_dn)
            if part == 0:
                z = z * (HEAD_DIM ** -0.5)
            o_ref = outs[3 * g + part]
            if d == 1:
                o_ref[0, 0] = z.astype(BF16)
            else:
                nc = gw // LANES
                for c in range(nc):
                    scr[c] = z[:, c * LANES:(c + 1) * LANES]
                for r in range(d):
                    o_ref[0, r] = jnp.concatenate(
                        [scr[c, pl.ds(r, tm // d, stride=d), :] for c in range(nc)], axis=1).astype(BF16)
    gate = _sigmoid(jnp.dot(h, w_ref[0, :, 9 * gw:9 * gw + D], preferred_element_type=F32))
    gate_ref[0] = gate.astype(BF16)


def _qkv(x, sc, sh, w_c, cos_t, sup_t, sdn_t, layer):
    B, S, D = x.shape
    tm = TOKEN_TILE
    gw = B_GROUP_WIDTH
    nw = w_c.shape[2]
    out_shapes, out_specs = [], []
    for d in DILATIONS:
        for _ in range(3):
            out_shapes.append(jax.ShapeDtypeStruct((B, d, S // d, gw), BF16))
            out_specs.append(pl.BlockSpec((1, d, tm // d, gw), lambda b, i: (b, 0, i, 0)))
    out_shapes.append(jax.ShapeDtypeStruct((B, S, D), BF16))
    out_specs.append(pl.BlockSpec((1, tm, D), lambda b, i: (b, i, 0)))
    return pl.pallas_call(
        _qkv_body,
        grid=(B, S // tm),
        in_specs=[
            pl.BlockSpec((1, tm, D), lambda b, i: (b, i, 0)),
            pl.BlockSpec((1, 1, D), lambda b, i: (b, 0, 0)),
            pl.BlockSpec((1, 1, D), lambda b, i: (b, 0, 0)),
            pl.BlockSpec((1, D, nw), lambda b, i: (layer, 0, 0), pipeline_mode=pl.Buffered(1)),
            pl.BlockSpec((tm, LANES), lambda b, i: (i, 0)),
            pl.BlockSpec((tm, LANES), lambda b, i: (i, 0)),
            pl.BlockSpec((tm, LANES), lambda b, i: (i, 0)),
        ],
        out_specs=out_specs,
        out_shape=out_shapes,
        scratch_shapes=[pltpu.VMEM((gw // LANES, tm, LANES), F32)],
        compiler_params=_params(("arbitrary", "arbitrary")),
        name="qkv",
    )(x, sc, sh, w_c, cos_t, sup_t, sdn_t)


def _attention_body(*refs):
    ins = refs[:15]
    o_ref = refs[15]
    o_scr, m_scr, l_scr, bias_scr, p_scr = refs[16:21]
    n = pl.program_id(1)
    blk = ATT_SPAN
    qi = lax.broadcasted_iota(I32, (blk, 2 * blk), 0)
    kj = lax.broadcasted_iota(I32, (blk, 2 * blk), 1)
    band = (kj >= qi) & (kj <= qi + ATT_SPAN)
    neg = jnp.float32(-jnp.inf)
    bias_scr[0] = jnp.where(band, 0.0, neg)
    bias_scr[1] = jnp.where(band & ((kj >= blk) | (n > 0)), 0.0, neg)
    lane = lax.broadcasted_iota(I32, (blk, LANES), 1)
    head0 = lane < HEAD_DIM
    ones = jnp.ones((2 * blk, LANES), BF16)

    def store(scr, g, d, r, j, val):
        view = scr.at[g, pl.ds(j * blk * d, blk * d)]
        if d == 1:
            view[...] = val
        else:
            view[pl.ds(r, blk, stride=d), :] = val

    def window(prev_ref, cur_ref, r, j):
        if j == 0:
            return jnp.concatenate([prev_ref[0, r], cur_ref[0, r, 0:blk]], axis=0)
        return cur_ref[0, r, (j - 1) * blk:(j + 1) * blk]

    chunk_id = 0
    for g, d in enumerate(DILATIONS):
        q_ref, kc_ref, kp_ref, vc_ref, vp_ref = ins[5 * g:5 * g + 5]
        nj = ATT_BLOCK // d // blk
        units = [(r, j) for r in range(d) for j in range(nj)]
        for c0 in range(0, len(units), ATT_CHUNK):
            chunk = units[c0:c0 + ATT_CHUNK]
            pbase = (chunk_id % 2) * 2 * ATT_CHUNK
            chunk_id += 1
            for ui, (r, j) in enumerate(chunk):
                q = q_ref[0, r, j * blk:(j + 1) * blk]
                kk = window(kp_ref, kc_ref, r, j)
                bias = bias_scr[1 if j == 0 else 0]
                zero = jnp.zeros_like(q)
                ms = []
                for h, qh in enumerate((jnp.where(head0, q, zero), jnp.where(head0, zero, q))):
                    s = lax.dot_general(qh, kk, (((1,), (1,)), ((), ())), preferred_element_type=F32) + bias
                    m = jnp.max(s, axis=1, keepdims=True)
                    p_scr[pbase + 2 * ui + h] = jnp.exp(s - m).astype(BF16)
                    ms.append(m)
                store(m_scr, g, d, r, j, jnp.where(head0, ms[0], ms[1]))
            for ui, (r, j) in enumerate(chunk):
                ve = jnp.concatenate([window(vp_ref, vc_ref, r, j), ones], axis=1)
                oe = [jnp.dot(p_scr[pbase + 2 * ui + h], ve, preferred_element_type=F32) for h in range(2)]
                store(o_scr, g, d, r, j, jnp.where(head0, oe[0][:, :LANES], oe[1][:, :LANES]))
                store(l_scr, g, d, r, j, jnp.where(head0, oe[0][:, LANES:], oe[1][:, LANES:]))

    def merge(c, carry):
        rows = pl.ds(pl.multiple_of(c * blk, blk), blk)
        ms = [m_scr[g, rows] for g in range(3)]
        m_ref = jnp.maximum(jnp.maximum(ms[0], ms[1]), ms[2])
        num = jnp.zeros((blk, LANES), F32)
        den = jnp.zeros((blk, LANES), F32)
        for g in range(3):
            w = jnp.exp(ms[g] - m_ref)
            num = num + w * o_scr[g, rows]
            den = den + w * l_scr[g, rows]
        o_ref[0, rows] = (num / den).astype(BF16)
        return carry
    lax.fori_loop(0, ATT_BLOCK // blk, merge, 0)


def _attention(qkv):
    B = qkv[0].shape[0]
    S = qkv[0].shape[2]
    gw = B_GROUP_WIDTH
    blk = ATT_SPAN
    nblk = S // ATT_BLOCK
    in_specs, args = [], []
    for g, d in enumerate(DILATIONS):
        rows = ATT_BLOCK // d
        per = rows // blk
        cur = pl.BlockSpec((1, d, rows, LANES), lambda b, n, hp: (b, 0, n, hp))
        prev = pl.BlockSpec((1, d, blk, LANES),
                            lambda b, n, hp, per=per: (b, 0, jnp.maximum(n * per - 1, 0), hp))
        q, k, v = qkv[3 * g:3 * g + 3]
        in_specs += [cur, cur, prev, cur, prev]
        args += [q, k, k, v, v]
    return pl.pallas_call(
        _attention_body,
        grid=(B, nblk, gw // LANES),
        in_specs=in_specs,
        out_specs=pl.BlockSpec((1, ATT_BLOCK, LANES), lambda b, n, hp: (b, n, hp)),
        out_shape=jax.ShapeDtypeStruct((B, S, gw), BF16),
        scratch_shapes=[pltpu.VMEM((3, ATT_BLOCK, LANES), F32)] * 3 + [
            pltpu.VMEM((2, blk, 2 * blk), F32),
            pltpu.VMEM((4 * ATT_CHUNK, blk, 2 * blk), BF16)],
        compiler_params=_params(("arbitrary", "arbitrary", "arbitrary")),
        name="attention",
    )(*args)


def _post_mixer_body(alpha, x_ref, ya_ref, gb_ref, ob_ref, wbb_ref, wout_ref, gm_ref, lng_ref, lnb_ref,
                     sc_ref, sh_ref, wr_ref, br_ref,
                     x1_ref, hp_ref, idx_ref, rank_ref, wts_ref, cnt_ref, carry):
    tt = x_ref.shape[1]
    first = (pl.program_id(0) == 0) & (pl.program_id(1) == 0)

    @pl.when(first)
    def _():
        carry[...] = jnp.zeros_like(carry)

    yb = jnp.dot(ob_ref[0], wbb_ref[0], preferred_element_type=F32)
    merged = (ya_ref[0].astype(F32) + gb_ref[0].astype(F32) * yb).astype(BF16)
    y = jnp.dot(merged, wout_ref[0], preferred_element_type=F32)
    x1 = _layer_norm(alpha * x_ref[0] + (1.0 + gm_ref[0]) * y, lng_ref[0], lnb_ref[0])
    x1_ref[0] = x1
    h2 = x1 * (1.0 + sc_ref[0]) + sh_ref[0]
    hp_ref[...] = _pack_bf16_pairs(h2)

    logits = lax.dot_general(wr_ref[0], h2, (((1,), (1,)), ((), ())), preferred_element_type=F32,
                             precision=lax.Precision.HIGHEST) + br_ref[0]
    eid = lax.broadcasted_iota(I32, (N_EXPERTS, tt), 0)
    vals, hots = [], []
    cur = logits
    for k in range(TOP_K):
        m = jnp.max(cur, axis=0, keepdims=True)
        sel = jnp.min(jnp.where(cur == m, eid, N_EXPERTS), axis=0, keepdims=True)
        hot = eid == sel
        idx_ref[k:k + 1, :] = sel
        vals.append(m)
        hots.append(hot)
        cur = jnp.where(hot, -jnp.inf, cur)
    ex = [jnp.exp(v - vals[0]) for v in vals]
    tot = ex[0] + ex[1] + ex[2] + ex[3]
    for k in range(TOP_K):
        wts_ref[k:k + 1, :] = ex[k] / tot

    member = jnp.zeros((N_EXPERTS, tt), F32)
    for hot in hots:
        member = member + hot.astype(F32)
    before = (lax.broadcasted_iota(I32, (tt, tt), 0) < lax.broadcasted_iota(I32, (tt, tt), 1))
    excl = jnp.dot(member.astype(BF16), before.astype(BF16), preferred_element_type=F32)
    base = carry[:, 0:1] + excl
    for k in range(TOP_K):
        rank_ref[k:k + 1, :] = jnp.sum(jnp.where(hots[k], base, 0.0), axis=0, keepdims=True).astype(I32)
    carry[...] = carry[...] + jnp.sum(member, axis=1, keepdims=True)
    cnt_ref[...] = carry[...]


def _post_mixer(x, ya, gb, ob, w_bb, w_out, gm, ln_g, ln_b, sc, sh, w_rt, b_r, layer, depth):
    B, S, D = x.shape
    tt = TOKEN_TILE
    nt = S // tt
    T = B * S
    gw = B_GROUP_WIDTH
    tok = lambda b, i: (b, i, 0)
    per_b = lambda b, i: (b, 0, 0)
    per_l3 = lambda b, i: (layer, 0, 0)
    flat = lambda b, i: (0, b * nt + i)
    return pl.pallas_call(
        functools.partial(_post_mixer_body, (2.0 * depth) ** 0.25),
        grid=(B, nt),
        in_specs=[
            pl.BlockSpec((1, tt, D), tok),
            pl.BlockSpec((1, tt, D), tok),
            pl.BlockSpec((1, tt, D), tok),
            pl.BlockSpec((1, tt, gw), tok),
            pl.BlockSpec((1, gw, D), per_l3),
            pl.BlockSpec((1, D, D), per_l3),
            pl.BlockSpec((1, 1, D), per_b),
            pl.BlockSpec((1, 1, D), per_l3),
            pl.BlockSpec((1, 1, D), per_l3),
            pl.BlockSpec((1, 1, D), per_b),
            pl.BlockSpec((1, 1, D), per_b),
            pl.BlockSpec((1, N_EXPERTS, D), per_l3),
            pl.BlockSpec((1, N_EXPERTS, 1), per_l3),
        ],
        out_specs=[
            pl.BlockSpec((1, tt, D), tok),
            pl.BlockSpec((tt, D // 2), lambda b, i: (b * nt + i, 0)),
            pl.BlockSpec((TOP_K, tt), flat),
            pl.BlockSpec((TOP_K, tt), flat),
            pl.BlockSpec((TOP_K, tt), flat),
            pl.BlockSpec((N_EXPERTS, LANES), lambda b, i: (0, 0)),
        ],
        out_shape=[
            jax.ShapeDtypeStruct((B, S, D), F32),
            jax.ShapeDtypeStruct((T, D // 2), U32),
            jax.ShapeDtypeStruct((TOP_K, T), I32),
            jax.ShapeDtypeStruct((TOP_K, T), I32),
            jax.ShapeDtypeStruct((TOP_K, T), F32),
            jax.ShapeDtypeStruct((N_EXPERTS, LANES), F32),
        ],
        scratch_shapes=[pltpu.VMEM((N_EXPERTS, LANES), F32)],
        compiler_params=_params(("arbitrary", "arbitrary")),
        name="post_mixer",
    )(x, ya, gb, ob, w_bb, w_out, gm, ln_g, ln_b, sc, sh, w_rt, b_r)


def _dispatch_body(ends_ref, padded_ref, h_ref, slot_ref, xs_ref, zbuf, sem):
    tt = h_ref.shape[0]
    tm = zbuf.shape[0]

    @pl.when(pl.program_id(0) == 0)
    def _():
        zbuf[...] = jnp.zeros_like(zbuf)
        for e in range(N_EXPERTS):
            @pl.when(padded_ref[e] > 0)
            def _():
                start = pl.multiple_of(ends_ref[e] - tm, tm)
                cp = pltpu.make_async_copy(zbuf, xs_ref.at[pl.ds(start, tm)], sem)
                cp.start()
                cp.wait()

        def fill_unused(t, carry):
            cp = pltpu.make_async_copy(zbuf, xs_ref.at[pl.ds(pl.multiple_of(t * tm, tm), tm)], sem)
            cp.start()
            cp.wait()
            return carry
        lax.fori_loop(ends_ref[N_EXPERTS - 1] // tm, xs_ref.shape[0] // tm, fill_unused, 0)

    def issue(j, carry):
        for k in range(TOP_K):
            pltpu.make_async_copy(h_ref.at[pl.ds(j, 1)], xs_ref.at[pl.ds(slot_ref[k, j], 1)], sem).start()
        return carry
    lax.fori_loop(0, tt, issue, 0)
    for k in range(TOP_K):
        pltpu.make_async_copy(h_ref, xs_ref.at[pl.ds(0, tt)], sem).wait()


def _dispatch(hp, slot, ends, padded, n_slots):
    T, W = hp.shape
    tt = TOKEN_TILE
    grid_spec = pltpu.PrefetchScalarGridSpec(
        num_scalar_prefetch=2,
        grid=(T // tt,),
        in_specs=[
            pl.BlockSpec((tt, W), lambda i, *_: (i, 0)),
            pl.BlockSpec((TOP_K, tt), lambda i, *_: (0, i), memory_space=pltpu.SMEM),
        ],
        out_specs=pl.BlockSpec(memory_space=pl.ANY),
        scratch_shapes=[pltpu.VMEM((EXPERT_TILE, W), U32), pltpu.SemaphoreType.DMA],
    )
    return pl.pallas_call(
        _dispatch_body,
        grid_spec=grid_spec,
        out_shape=jax.ShapeDtypeStruct((n_slots, W), U32),
        compiler_params=pltpu.CompilerParams(dimension_semantics=("arbitrary",), vmem_limit_bytes=VMEM_LIMIT,
                                             has_side_effects=True),
        name="dispatch",
    )(ends, padded, hp, slot)


def _deinterleave_body(w_ref, o_ref):
    n = 2 * LANES
    src = lax.broadcasted_iota(I32, (n, n), 0)
    dst = lax.broadcasted_iota(I32, (n, n), 1)
    want = jnp.where(dst < LANES, 2 * dst, 2 * (dst - LANES) + 1)
    perm = (src == want).astype(BF16)
    for blk in range(w_ref.shape[2] // n):
        cols = slice(blk * n, (blk + 1) * n)
        o_ref[0, :, cols] = jnp.dot(w_ref[0, :, cols].astype(BF16), perm,
                                    preferred_element_type=F32).astype(BF16)


def _deinterleave_up(w_up):
    N, D, F2 = w_up.shape
    return pl.pallas_call(
        _deinterleave_body,
        grid=(N,),
        in_specs=[pl.BlockSpec((1, D, F2), lambda e: (e, 0, 0))],
        out_specs=pl.BlockSpec((1, D, F2), lambda e: (e, 0, 0)),
        out_shape=jax.ShapeDtypeStruct((N, D, F2), BF16),
        compiler_params=_params(("arbitrary",)),
        name="deinterleave_up",
    )(w_up)


def _experts_body(te_ref, nact_ref, xs_ref, wup_ref, bup_ref, wdn_ref, bdn_ref, ys_ref):
    i = pl.program_id(0)
    ff = wdn_ref.shape[1]

    @pl.when(i < nact_ref[0])
    def _():
        wd = xs_ref[...]
        x = jnp.concatenate([_unpack_hi(wd), _unpack_lo(wd)], axis=1).astype(BF16)
        hh = jnp.dot(x, wup_ref[0], preferred_element_type=F32) + bup_ref[0]
        acts = []
        for blk in range(ff // LANES):
            glu = jnp.minimum(hh[:, 2 * blk * LANES:(2 * blk + 1) * LANES], SWIGLU_LIMIT)
            lin = jnp.clip(hh[:, (2 * blk + 1) * LANES:(2 * blk + 2) * LANES], -SWIGLU_LIMIT, SWIGLU_LIMIT)
            acts.append((glu * _sigmoid(SWIGLU_ALPHA * glu) * (lin + 1.0)).astype(BF16))
        act = jnp.concatenate(acts, axis=1)
        y = jnp.dot(act, wdn_ref[0], preferred_element_type=F32) + bdn_ref[0]
        ys_ref[...] = _pack_bf16_pairs(y)

    @pl.when(i >= nact_ref[0])
    def _():
        ys_ref[...] = jnp.zeros_like(ys_ref)


def _experts(xs, tile_expert, n_active, w_up, b_up, w_dn, b_dn, layer):
    P, W = xs.shape
    tm = EXPERT_TILE
    D = 2 * W
    ff2 = w_up.shape[2]
    ff = w_dn.shape[1]
    ex = lambda i, te, na: (layer * N_EXPERTS + te[i], 0, 0)
    grid_spec = pltpu.PrefetchScalarGridSpec(
        num_scalar_prefetch=2,
        grid=(P // tm,),
        in_specs=[
            pl.BlockSpec((tm, W), lambda i, te, na: (jnp.where(i < na[0], i, 0), 0)),
            pl.BlockSpec((1, D, ff2), ex),
            pl.BlockSpec((1, 1, ff2), ex),
            pl.BlockSpec((1, ff, D), ex),
            pl.BlockSpec((1, 1, D), ex),
        ],
        out_specs=pl.BlockSpec((tm, W), lambda i, te, na: (i, 0)),
    )
    return pl.pallas_call(
        _experts_body,
        grid_spec=grid_spec,
        out_shape=jax.ShapeDtypeStruct((P, W), U32),
        compiler_params=_params(("arbitrary",)),
        name="experts",
    )(tile_expert, n_active, xs, w_up, b_up, w_dn, b_dn)


def _combine_body(alpha, x_ref, slot_ref, wts_ref, ys_ref, gm_ref, lng_ref, lnb_ref, o_ref, ybuf, sem):
    tt = x_ref.shape[1]

    def issue(j, carry):
        for k in range(TOP_K):
            pltpu.make_async_copy(ys_ref.at[pl.ds(slot_ref[k, j], 1)], ybuf.at[k, pl.ds(j, 1)], sem).start()
        return carry
    lax.fori_loop(0, tt, issue, 0)
    for k in range(TOP_K):
        pltpu.make_async_copy(ys_ref.at[pl.ds(0, tt)], ybuf.at[k], sem).wait()

    hi = jnp.zeros((tt, ybuf.shape[2]), F32)
    lo = jnp.zeros((tt, ybuf.shape[2]), F32)
    for k in range(TOP_K):
        w = wts_ref[:, k:k + 1]
        wd = ybuf[k]
        hi = hi + w * _unpack_hi(wd)
        lo = lo + w * _unpack_lo(wd)
    y = jnp.concatenate([hi, lo], axis=1)
    o_ref[0] = _layer_norm(alpha * x_ref[0] + (1.0 + gm_ref[0]) * y, lng_ref[0], lnb_ref[0])


def _combine(x1, slot, wts_t, ys, gm, ln_g, ln_b, layer, depth):
    B, S, D = x1.shape
    tt = TOKEN_TILE
    nt = S // tt
    return pl.pallas_call(
        functools.partial(_combine_body, (2.0 * depth) ** 0.25),
        grid=(B, nt),
        in_specs=[
            pl.BlockSpec((1, tt, D), lambda b, i: (b, i, 0)),
            pl.BlockSpec((TOP_K, tt), lambda b, i: (0, b * nt + i), memory_space=pltpu.SMEM),
            pl.BlockSpec((tt, TOP_K), lambda b, i: (b * nt + i, 0)),
            pl.BlockSpec(memory_space=pl.ANY),
            pl.BlockSpec((1, 1, D), lambda b, i: (b, 0, 0)),
            pl.BlockSpec((1, 1, D), lambda b, i: (layer, 0, 0)),
            pl.BlockSpec((1, 1, D), lambda b, i: (layer, 0, 0)),
        ],
        out_specs=pl.BlockSpec((1, tt, D), lambda b, i: (b, i, 0)),
        out_shape=jax.ShapeDtypeStruct((B, S, D), F32),
        scratch_shapes=[pltpu.VMEM((TOP_K, tt, D // 2), U32), pltpu.SemaphoreType.DMA],
        compiler_params=_params(("arbitrary", "arbitrary")),
        name="combine",
    )(x1, slot, wts_t, ys, gm, ln_g, ln_b)


def _rope_tables(S):
    half = ROPE_DIM // 2
    inv = ROPE_THETA ** (-jnp.arange(half, dtype=F32) / half)
    ang = jnp.arange(S, dtype=F32)[:, None] * inv[None, :]
    cos, sin = jnp.cos(ang), jnp.sin(ang)
    lane = jnp.arange(LANES) % HEAD_DIM
    pick = lane % half
    cos_l, sin_l = cos[:, pick], sin[:, pick]
    cos_t = jnp.where(lane < ROPE_DIM, cos_l, 1.0)
    s_up = jnp.where((lane >= half) & (lane < ROPE_DIM), sin_l, 0.0)
    s_dn = jnp.where(lane < half, -sin_l, 0.0)
    return cos_t.astype(F32), s_up.astype(F32), s_dn.astype(F32)


def kernel(x, c, w_ada, b_ada, w_in, spatial_w, spatial_b, w_branch_a, w_branch_b, w_out, ln1_g, ln1_b,
           w_router, b_router, w_up, b_up, w_down, b_down, ln2_g, ln2_b):
    B, S, D = x.shape
    L = w_ada.shape[0]
    T = B * S
    gw = B_GROUP_WIDTH
    nq = len(DILATIONS) * gw
    assert S % ATT_BLOCK == 0 and S % TOKEN_TILE == 0 and D % (2 * LANES) == 0

    q0, k0, v0, g0 = 2 * D, 2 * D + nq, 2 * D + 2 * nq, 2 * D + 3 * nq
    w_a = jnp.concatenate([w_in[:, :, :2 * D], w_in[:, :, g0:g0 + D]], axis=2).astype(BF16)
    cols = []
    for g in range(len(DILATIONS)):
        for base in (q0, k0, v0):
            cols.append(w_in[:, :, base + g * gw:base + (g + 1) * gw])
    cols.append(w_in[:, :, g0 + D:g0 + 2 * D])
    w_c = jnp.concatenate(cols, axis=2).astype(BF16)
    w_ba = w_branch_a.astype(BF16)
    w_bb = w_branch_b.astype(BF16)
    w_o = w_out.astype(BF16)
    sp_b = spatial_b[..., None]
    w_rt = jnp.swapaxes(w_router, 1, 2)
    b_r = b_router[..., None]
    ff = w_down.shape[2]
    w_up_p = _deinterleave_up(w_up.reshape(L * N_EXPERTS, D, 2 * ff))
    b_up_p = jnp.swapaxes(b_up.reshape(L * N_EXPERTS, 1, ff // LANES, LANES, 2), 3, 4).reshape(
        L * N_EXPERTS, 1, 2 * ff)
    w_dn_p = w_down.astype(BF16).reshape(L * N_EXPERTS, ff, D)
    b_dn_p = b_down.reshape(L * N_EXPERTS, 1, D)
    ln1g, ln1b = ln1_g[:, None, :], ln1_b[:, None, :]
    ln2g, ln2b = ln2_g[:, None, :], ln2_b[:, None, :]
    cos_t, sup_t, sdn_t = _rope_tables(S)

    mod = _ada_mod(c, w_ada, b_ada)
    n_slots = T * TOP_K + N_EXPERTS * EXPERT_TILE
    n_tiles = n_slots // EXPERT_TILE

    for l in range(L):
        sh1, sc1, g1, sh2, sc2, g2 = [mod[l, :, None, i * D:(i + 1) * D] for i in range(6)]
        ya = _branch_a(x, sc1, sh1, w_a, spatial_w, sp_b, w_ba, l)
        *qkv, gb = _qkv(x, sc1, sh1, w_c, cos_t, sup_t, sdn_t, l)
        ob = _attention(qkv)
        x1, hp, idx, rank, wts, cnt = _post_mixer(x, ya, gb, ob, w_bb, w_o, g1, ln1g, ln1b, sc2, sh2,
                                                  w_rt, b_r, l, L)
        counts = cnt[:, 0].astype(I32)
        padded = ((counts + EXPERT_TILE - 1) // EXPERT_TILE) * EXPERT_TILE
        ends = jnp.cumsum(padded)
        starts = ends - padded
        slot = rank + jnp.sum(jnp.where(idx[None] == jnp.arange(N_EXPERTS, dtype=I32)[:, None, None],
                                        starts[:, None, None], 0), axis=0)
        n_active = (ends[-1] // EXPERT_TILE).astype(I32)
        tile_start = jnp.arange(n_tiles, dtype=I32) * EXPERT_TILE
        te = jnp.sum((ends[None, :] <= tile_start[:, None]).astype(I32), axis=1)
        te_last = jnp.sum((ends <= (n_active - 1) * EXPERT_TILE).astype(I32))
        te = jnp.minimum(jnp.where(tile_start < ends[-1], te, te_last), N_EXPERTS - 1).astype(I32)

        xs = _dispatch(hp, slot, ends.astype(I32), padded.astype(I32), n_slots)
        ys = _experts(xs, te, n_active.reshape(1), w_up_p, b_up_p, w_dn_p, b_dn_p, l)
        x = _combine(x1, slot, wts.T, ys, g2, ln2g, ln2b, l, L)
    return x
```

```python
import functools
import math

import jax
import jax.numpy as jnp
from jax import lax
from jax.experimental import pallas as pl
from jax.experimental.pallas import tpu as pltpu

BF16 = jnp.bfloat16
F32 = jnp.float32
U32 = jnp.uint32
I32 = jnp.int32

A_CHUNK = 128
A_GROUPS = 4
HEAD_DIM = 64
B_HEADS = 8
DILATIONS = (1, 4, 16)
ATT_SPAN = 128
B_GROUP_WIDTH = B_HEADS * HEAD_DIM
ROPE_DIM = HEAD_DIM // 4
ROPE_THETA = 500000.0
N_EXPERTS = 32
TOP_K = 4
SWIGLU_LIMIT = 7.0
SWIGLU_ALPHA = 1.702
LN_EPS = 1e-5

LANES = 128
VMEM_LIMIT = 56 * 1024 * 1024
TOKEN_TILE = 512
ATT_BLOCK = 128 * DILATIONS[-1]
EXPERT_TILE = 512
ATT_CHUNK = 4
RUN_BITS = TOKEN_TILE.bit_length()
RUN_ALIGN = 8
RUN_TILE_ROWS = TOP_K * TOKEN_TILE + N_EXPERTS * RUN_ALIGN
PERM_ROWS = 256


def _gelu(x):
    return 0.5 * x * (1.0 + jnp.tanh(math.sqrt(2.0 / math.pi) * (x + 0.044715 * (x * x * x))))


def _sigmoid(x):
    return 1.0 / (1.0 + jnp.exp(-x))


def _layer_norm(r, g, b):
    mu = jnp.mean(r, axis=-1, keepdims=True)
    c = r - mu
    var = jnp.mean(c * c, axis=-1, keepdims=True)
    return c * lax.rsqrt(var + LN_EPS) * g + b


def _pack_bf16_pairs(x):
    w = x.shape[1] // 2
    bits = lax.bitcast_convert_type(x.astype(BF16).astype(F32), U32)
    return bits[:, :w] | lax.shift_right_logical(bits[:, w:], jnp.uint32(16))


def _unpack_hi(wd):
    return lax.bitcast_convert_type(wd & jnp.uint32(0xFFFF0000), F32)


def _unpack_lo(wd):
    return lax.bitcast_convert_type(lax.shift_left(wd, jnp.uint32(16)), F32)


def _params(sem):
    return pltpu.CompilerParams(dimension_semantics=sem, vmem_limit_bytes=VMEM_LIMIT)


def _ada_body(c_ref, w_ref, b_ref, o_ref):
    o_ref[0] = jnp.dot(c_ref[...], w_ref[0], preferred_element_type=F32,
                       precision=lax.Precision.HIGHEST) + b_ref[0]


def _ada_mod(c, w_ada, b_ada):
    L, D, N = w_ada.shape
    B = c.shape[0]
    tn = D
    return pl.pallas_call(
        _ada_body,
        grid=(L, N // tn),
        in_specs=[
            pl.BlockSpec((B, D), lambda l, j: (0, 0)),
            pl.BlockSpec((1, D, tn), lambda l, j: (l, 0, j)),
            pl.BlockSpec((1, 1, tn), lambda l, j: (l, 0, j)),
        ],
        out_specs=pl.BlockSpec((1, B, tn), lambda l, j: (l, 0, j)),
        out_shape=jax.ShapeDtypeStruct((L, B, N), F32),
        compiler_params=_params(("arbitrary", "arbitrary")),
        name="ada_mod",
    )(c, w_ada, b_ada.reshape(L, 1, N))


def _branch_a_body(x_ref, sc_ref, sh_ref, wa_ref, spw_ref, spb_ref, wba_ref, o_ref, a_scr):
    tm, D = x_ref.shape[1], x_ref.shape[2]
    gd = D // A_GROUPS
    h = (x_ref[0] * (1.0 + sc_ref[0]) + sh_ref[0]).astype(BF16)
    row = lax.broadcasted_iota(I32, (A_CHUNK, A_CHUNK), 0)
    col = lax.broadcasted_iota(I32, (A_CHUNK, A_CHUNK), 1)
    causal = row >= col
    for g in range(A_GROUPS):
        v = _gelu(jnp.dot(h, wa_ref[0, :, D + g * gd:D + (g + 1) * gd], preferred_element_type=F32))
        mu = jnp.mean(v, axis=-1, keepdims=True)
        vc = v - mu
        var = jnp.mean(vc * vc, axis=-1, keepdims=True)
        vn = (vc * lax.rsqrt(var + LN_EPS)).astype(BF16)
        u = _gelu(jnp.dot(h, wa_ref[0, :, g * gd:(g + 1) * gd], preferred_element_type=F32))
        ws = jnp.where(causal, spw_ref[0, g], 0.0).astype(BF16)
        bias = spb_ref[0, g]
        for c in range(tm // A_CHUNK):
            rows = slice(c * A_CHUNK, (c + 1) * A_CHUNK)
            sv = jnp.dot(ws, vn[rows], preferred_element_type=F32) + bias
            a_scr[rows, g * gd:(g + 1) * gd] = (u[rows] * sv).astype(BF16)
    ya = jnp.dot(a_scr[...], wba_ref[0], preferred_element_type=F32)
    gate = _sigmoid(jnp.dot(h, wa_ref[0, :, 2 * D:3 * D], preferred_element_type=F32))
    o_ref[0] = (gate * ya).astype(BF16)


def _branch_a(x, sc, sh, w_a, sp_w, sp_b, w_ba, layer):
    B, S, D = x.shape
    tm = TOKEN_TILE
    return pl.pallas_call(
        _branch_a_body,
        grid=(B, S // tm),
        in_specs=[
            pl.BlockSpec((1, tm, D), lambda b, i: (b, i, 0)),
            pl.BlockSpec((1, 1, D), lambda b, i: (b, 0, 0)),
            pl.BlockSpec((1, 1, D), lambda b, i: (b, 0, 0)),
            pl.BlockSpec((1, D, 3 * D), lambda b, i: (layer, 0, 0)),
            pl.BlockSpec((1, A_GROUPS, A_CHUNK, A_CHUNK), lambda b, i: (layer, 0, 0, 0)),
            pl.BlockSpec((1, A_GROUPS, A_CHUNK, 1), lambda b, i: (layer, 0, 0, 0)),
            pl.BlockSpec((1, D, D), lambda b, i: (layer, 0, 0)),
        ],
        out_specs=pl.BlockSpec((1, tm, D), lambda b, i: (b, i, 0)),
        out_shape=jax.ShapeDtypeStruct((B, S, D), BF16),
        scratch_shapes=[pltpu.VMEM((tm, D), BF16)],
        compiler_params=_params(("arbitrary", "arbitrary")),
        name="branch_a",
    )(x, sc, sh, w_a, sp_w, sp_b, w_ba)


def _rope(z, cos, s_up, s_dn):
    half = ROPE_DIM // 2
    parts = []
    for c in range(z.shape[1] // LANES):
        zc = z[:, c * LANES:(c + 1) * LANES]
        parts.append(zc * cos + pltpu.roll(zc, half, 1) * s_up + pltpu.roll(zc, LANES - half, 1) * s_dn)
    return jnp.concatenate(parts, axis=1)


def _qkv_body(x_ref, sc_ref, sh_ref, w_ref, cos_ref, sup_ref, sdn_ref, *rest):
    outs = rest[:9]
    gate_ref = rest[9]
    scr = rest[10]
    tm, D = x_ref.shape[1], x_ref.shape[2]
    gw = B_GROUP_WIDTH
    h = (x_ref[0] * (1.0 + sc_ref[0]) + sh_ref[0]).astype(BF16)
    cos, s_up, s_dn = cos_ref[...], sup_ref[...], sdn_ref[...]
    for g, d in enumerate(DILATIONS):
        for part in range(3):
            col = (3 * g + part) * gw
            z = jnp.dot(h, w_ref[0, :, col:col + gw], preferred_element_type=F32)
            if part < 2:
                z = _rope(z, cos, s_up, s_dn)
            if part == 0:
                z = z * (HEAD_DIM ** -0.5)
            o_ref = outs[3 * g + part]
            if d == 1:
                o_ref[0, 0] = z.astype(BF16)
            else:
                nc = gw // LANES
                for c in range(nc):
                    scr[c] = z[:, c * LANES:(c + 1) * LANES]
                for r in range(d):
                    o_ref[0, r] = jnp.concatenate(
                        [scr[c, pl.ds(r, tm // d, stride=d), :] for c in range(nc)], axis=1).astype(BF16)
    gate = _sigmoid(jnp.dot(h, w_ref[0, :, 9 * gw:9 * gw + D], preferred_element_type=F32))
    gate_ref[0] = gate.astype(BF16)


def _qkv(x, sc, sh, w_c, cos_t, sup_t, sdn_t, layer):
    B, S, D = x.shape
    tm = TOKEN_TILE
    gw = B_GROUP_WIDTH
    nw = w_c.shape[2]
    out_shapes, out_specs = [], []
    for d in DILATIONS:
        for _ in range(3):
            out_shapes.append(jax.ShapeDtypeStruct((B, d, S // d, gw), BF16))
            out_specs.append(pl.BlockSpec((1, d, tm // d, gw), lambda b, i: (b, 0, i, 0)))
    out_shapes.append(jax.ShapeDtypeStruct((B, S, D), BF16))
    out_specs.append(pl.BlockSpec((1, tm, D), lambda b, i: (b, i, 0)))
    return pl.pallas_call(
        _qkv_body,
        grid=(B, S // tm),
        in_specs=[
            pl.BlockSpec((1, tm, D), lambda b, i: (b, i, 0)),
            pl.BlockSpec((1, 1, D), lambda b, i: (b, 0, 0)),
            pl.BlockSpec((1, 1, D), lambda b, i: (b, 0, 0)),
            pl.BlockSpec((1, D, nw), lambda b, i: (layer, 0, 0), pipeline_mode=pl.Buffered(1)),
            pl.BlockSpec((tm, LANES), lambda b, i: (i, 0)),
            pl.BlockSpec((tm, LANES), lambda b, i: (i, 0)),
            pl.BlockSpec((tm, LANES), lambda b, i: (i, 0)),
        ],
        out_specs=out_specs,
        out_shape=out_shapes,
        scratch_shapes=[pltpu.VMEM((gw // LANES, tm, LANES), F32)],
        compiler_params=_params(("arbitrary", "arbitrary")),
        name="qkv",
    )(x, sc, sh, w_c, cos_t, sup_t, sdn_t)


def _attention_body(*refs):
    ins = refs[:15]
    o_ref = refs[15]
    o_scr, m_scr, l_scr, bias_scr, p_scr = refs[16:21]
    n = pl.program_id(1)
    blk = ATT_SPAN
    qi = lax.broadcasted_iota(I32, (blk, 2 * blk), 0)
    kj = lax.broadcasted_iota(I32, (blk, 2 * blk), 1)
    band = (kj >= qi) & (kj <= qi + ATT_SPAN)
    neg = jnp.float32(-jnp.inf)
    bias_scr[0] = jnp.where(band, 0.0, neg)
    bias_scr[1] = jnp.where(band & ((kj >= blk) | (n > 0)), 0.0, neg)
    lane = lax.broadcasted_iota(I32, (blk, LANES), 1)
    head0 = lane < HEAD_DIM
    ones = jnp.ones((2 * blk, LANES), BF16)

    def store(scr, g, d, r, j, val):
        view = scr.at[g, pl.ds(j * blk * d, blk * d)]
        if d == 1:
            view[...] = val
        else:
            view[pl.ds(r, blk, stride=d), :] = val

    def window(prev_ref, cur_ref, r, j):
        if j == 0:
            return jnp.concatenate([prev_ref[0, r], cur_ref[0, r, 0:blk]], axis=0)
        return cur_ref[0, r, (j - 1) * blk:(j + 1) * blk]

    chunk_id = 0
    for g, d in enumerate(DILATIONS):
        q_ref, kc_ref, kp_ref, vc_ref, vp_ref = ins[5 * g:5 * g + 5]
        nj = ATT_BLOCK // d // blk
        units = [(r, j) for r in range(d) for j in range(nj)]
        for c0 in range(0, len(units), ATT_CHUNK):
            chunk = units[c0:c0 + ATT_CHUNK]
            pbase = (chunk_id % 2) * 2 * ATT_CHUNK
            chunk_id += 1
            for ui, (r, j) in enumerate(chunk):
                q = q_ref[0, r, j * blk:(j + 1) * blk]
                kk = window(kp_ref, kc_ref, r, j)
                bias = bias_scr[1 if j == 0 else 0]
                zero = jnp.zeros_like(q)
                ms = []
                for h, qh in enumerate((jnp.where(head0, q, zero), jnp.where(head0, zero, q))):
                    s = lax.dot_general(qh, kk, (((1,), (1,)), ((), ())), preferred_element_type=F32) + bias
                    m = jnp.max(s, axis=1, keepdims=True)
                    p_scr[pbase + 2 * ui + h] = jnp.exp(s - m).astype(BF16)
                    ms.append(m)
                store(m_scr, g, d, r, j, jnp.where(head0, ms[0], ms[1]))
            for ui, (r, j) in enumerate(chunk):
                ve = jnp.concatenate([window(vp_ref, vc_ref, r, j), ones], axis=1)
                oe = [jnp.dot(p_scr[pbase + 2 * ui + h], ve, preferred_element_type=F32) for h in range(2)]
                store(o_scr, g, d, r, j, jnp.where(head0, oe[0][:, :LANES], oe[1][:, :LANES]))
                store(l_scr, g, d, r, j, jnp.where(head0, oe[0][:, LANES:], oe[1][:, LANES:]))

    def merge(c, carry):
        rows = pl.ds(pl.multiple_of(c * blk, blk), blk)
        ms = [m_scr[g, rows] for g in range(3)]
        m_ref = jnp.maximum(jnp.maximum(ms[0], ms[1]), ms[2])
        num = jnp.zeros((blk, LANES), F32)
        den = jnp.zeros((blk, LANES), F32)
        for g in range(3):
            w = jnp.exp(ms[g] - m_ref)
            num = num + w * o_scr[g, rows]
            den = den + w * l_scr[g, rows]
        o_ref[0, rows] = (num / den).astype(BF16)
        return carry
    lax.fori_loop(0, ATT_BLOCK // blk, merge, 0)


def _attention(qkv):
    B = qkv[0].shape[0]
    S = qkv[0].shape[2]
    gw = B_GROUP_WIDTH
    blk = ATT_SPAN
    nblk = S // ATT_BLOCK
    in_specs, args = [], []
    for g, d in enumerate(DILATIONS):
        rows = ATT_BLOCK // d
        per = rows // blk
        cur = pl.BlockSpec((1, d, rows, LANES), lambda b, n, hp: (b, 0, n, hp))
        prev = pl.BlockSpec((1, d, blk, LANES),
                            lambda b, n, hp, per=per: (b, 0, jnp.maximum(n * per - 1, 0), hp))
        q, k, v = qkv[3 * g:3 * g + 3]
        in_specs += [cur, cur, prev, cur, prev]
        args += [q, k, k, v, v]
    return pl.pallas_call(
        _attention_body,
        grid=(B, nblk, gw // LANES),
        in_specs=in_specs,
        out_specs=pl.BlockSpec((1, ATT_BLOCK, LANES), lambda b, n, hp: (b, n, hp)),
        out_shape=jax.ShapeDtypeStruct((B, S, gw), BF16),
        scratch_shapes=[pltpu.VMEM((3, ATT_BLOCK, LANES), F32)] * 3 + [
            pltpu.VMEM((2, blk, 2 * blk), F32),
            pltpu.VMEM((4 * ATT_CHUNK, blk, 2 * blk), BF16)],
        compiler_params=_params(("arbitrary", "arbitrary", "arbitrary")),
        name="attention",
    )(*args)


def _post_mixer_body(alpha, x_ref, ya_ref, gb_ref, ob_ref, wbb_ref, wout_ref, gm_ref, lng_ref, lnb_ref,
                     sc_ref, sh_ref, wr_ref, br_ref,
                     x1_ref, hc_ref, pos_ref, wts_ref, cnt_ref):
    tt = x_ref.shape[1]
    yb = jnp.dot(ob_ref[0], wbb_ref[0], preferred_element_type=F32)
    merged = (ya_ref[0].astype(F32) + gb_ref[0].astype(F32) * yb).astype(BF16)
    y = jnp.dot(merged, wout_ref[0], preferred_element_type=F32)
    x1 = _layer_norm(alpha * x_ref[0] + (1.0 + gm_ref[0]) * y, lng_ref[0], lnb_ref[0])
    x1_ref[0] = x1
    h2 = x1 * (1.0 + sc_ref[0]) + sh_ref[0]

    logits = lax.dot_general(wr_ref[0], h2, (((1,), (1,)), ((), ())), preferred_element_type=F32,
                             precision=lax.Precision.HIGHEST) + br_ref[0]
    eid = lax.broadcasted_iota(I32, (N_EXPERTS, tt), 0)
    vals, hots = [], []
    cur = logits
    for k in range(TOP_K):
        m = jnp.max(cur, axis=0, keepdims=True)
        sel = jnp.min(jnp.where(cur == m, eid, N_EXPERTS), axis=0, keepdims=True)
        hot = eid == sel
        vals.append(m)
        hots.append(hot)
        cur = jnp.where(hot, -jnp.inf, cur)
    ex = [jnp.exp(v - vals[0]) for v in vals]
    tot = ex[0] + ex[1] + ex[2] + ex[3]
    for k in range(TOP_K):
        wts_ref[k:k + 1, :] = ex[k] / tot

    member = jnp.zeros((N_EXPERTS, tt), F32)
    for hot in hots:
        member = member + hot.astype(F32)
    before = (lax.broadcasted_iota(I32, (tt, tt), 0) < lax.broadcasted_iota(I32, (tt, tt), 1))
    excl = jnp.dot(member.astype(BF16), before.astype(BF16), preferred_element_type=F32)
    cnt = jnp.broadcast_to(jnp.sum(member, axis=1, keepdims=True), (N_EXPERTS, LANES))
    cnt_ref[...] = cnt
    lower = (lax.broadcasted_iota(I32, (N_EXPERTS, N_EXPERTS), 1)
             < lax.broadcasted_iota(I32, (N_EXPERTS, N_EXPERTS), 0)).astype(F32)
    cnt_al = jnp.ceil(cnt * (1.0 / RUN_ALIGN)) * RUN_ALIGN
    off = jnp.dot(lower, cnt_al, preferred_element_type=F32, precision=lax.Precision.HIGHEST)[:, 0:1]
    base = off + excl
    pos = []
    for k in range(TOP_K):
        pk = jnp.sum(jnp.where(hots[k], base, 0.0), axis=0, keepdims=True).astype(I32)
        pos_ref[k:k + 1, :] = pk
        pos.append(pk)

    h2b = h2.astype(BF16)
    for c in range(hc_ref.shape[0] // PERM_ROWS):
        p_id = lax.broadcasted_iota(I32, (PERM_ROWS, tt), 0) + c * PERM_ROWS
        sel = (p_id == pos[0]) | (p_id == pos[1]) | (p_id == pos[2]) | (p_id == pos[3])
        rows = jnp.dot(jnp.where(sel, 1.0, 0.0).astype(BF16), h2b, preferred_element_type=F32)
        hc_ref[c * PERM_ROWS:(c + 1) * PERM_ROWS, :] = _pack_bf16_pairs(rows)


def _post_mixer(x, ya, gb, ob, w_bb, w_out, gm, ln_g, ln_b, sc, sh, w_rt, b_r, layer, depth):
    B, S, D = x.shape
    tt = TOKEN_TILE
    nt = S // tt
    T = B * S
    gw = B_GROUP_WIDTH
    tok = lambda b, i: (b, i, 0)
    per_b = lambda b, i: (b, 0, 0)
    per_l3 = lambda b, i: (layer, 0, 0)
    flat = lambda b, i: (0, b * nt + i)
    return pl.pallas_call(
        functools.partial(_post_mixer_body, (2.0 * depth) ** 0.25),
        grid=(B, nt),
        in_specs=[
            pl.BlockSpec((1, tt, D), tok),
            pl.BlockSpec((1, tt, D), tok),
            pl.BlockSpec((1, tt, D), tok),
            pl.BlockSpec((1, tt, gw), tok),
            pl.BlockSpec((1, gw, D), per_l3),
            pl.BlockSpec((1, D, D), per_l3),
            pl.BlockSpec((1, 1, D), per_b),
            pl.BlockSpec((1, 1, D), per_l3),
            pl.BlockSpec((1, 1, D), per_l3),
            pl.BlockSpec((1, 1, D), per_b),
            pl.BlockSpec((1, 1, D), per_b),
            pl.BlockSpec((1, N_EXPERTS, D), per_l3),
            pl.BlockSpec((1, N_EXPERTS, 1), per_l3),
        ],
        out_specs=[
            pl.BlockSpec((1, tt, D), tok),
            pl.BlockSpec((RUN_TILE_ROWS, D // 2), lambda b, i: (b * nt + i, 0)),
            pl.BlockSpec((TOP_K, tt), flat),
            pl.BlockSpec((TOP_K, tt), flat),
            pl.BlockSpec((N_EXPERTS, LANES), lambda b, i: (b * nt + i, 0)),
        ],
        out_shape=[
            jax.ShapeDtypeStruct((B, S, D), F32),
            jax.ShapeDtypeStruct((B * nt * RUN_TILE_ROWS, D // 2), U32),
            jax.ShapeDtypeStruct((TOP_K, T), I32),
            jax.ShapeDtypeStruct((TOP_K, T), F32),
            jax.ShapeDtypeStruct((B * nt * N_EXPERTS, LANES), F32),
        ],
        compiler_params=_params(("arbitrary", "arbitrary")),
        name="post_mixer",
    )(x, ya, gb, ob, w_bb, w_out, gm, ln_g, ln_b, sc, sh, w_rt, b_r)


def _run_pieces(n, fn):
    for bit in reversed(range(RUN_ALIGN.bit_length() - 1, RUN_BITS)):
        size = 1 << bit
        done = pl.multiple_of((n >> (bit + 1)) << (bit + 1), RUN_ALIGN)

        @pl.when((n & size) != 0)
        def _():
            fn(done, size)


def _run_copy_body(to_slots, src_off_ref, dst_off_ref, len_ref, ends_ref, padded_ref, tail_off_ref, tail_len_ref,
                   src_ref, dst_ref, zbuf, sem):
    t = pl.program_id(0)
    tm = zbuf.shape[0]

    @pl.when(t == 0)
    def _():
        zbuf[...] = jnp.zeros_like(zbuf)

    if to_slots:
        @pl.when(t == 0)
        def _():
            for e in range(N_EXPERTS):
                @pl.when(padded_ref[e] > 0)
                def _():
                    start = pl.multiple_of(ends_ref[e] - tm, tm)
                    cp = pltpu.make_async_copy(zbuf, dst_ref.at[pl.ds(start, tm)], sem)
                    cp.start()
                    cp.wait()

            def fill_unused(i, carry):
                cp = pltpu.make_async_copy(zbuf, dst_ref.at[pl.ds(pl.multiple_of(i * tm, tm), tm)], sem)
                cp.start()
                cp.wait()
                return carry
            lax.fori_loop(ends_ref[N_EXPERTS - 1] // tm, dst_ref.shape[0] // tm, fill_unused, 0)

    def run_copy(s0, d0, off, size):
        return pltpu.make_async_copy(src_ref.at[pl.ds(pl.multiple_of(s0 + off, RUN_ALIGN), size)],
                                     dst_ref.at[pl.ds(pl.multiple_of(d0 + off, RUN_ALIGN), size)], sem)

    def tail_copy(d0, off, size):
        return pltpu.make_async_copy(zbuf.at[pl.ds(0, size)],
                                     dst_ref.at[pl.ds(pl.multiple_of(d0 + off, RUN_ALIGN), size)], sem)

    def each_run(action):
        def run(e, carry):
            i = t * N_EXPERTS + e
            s0, d0 = src_off_ref[i], dst_off_ref[i]
            _run_pieces(len_ref[i], lambda off, size: action(run_copy(s0, d0, off, size)))
            return carry
        lax.fori_loop(0, N_EXPERTS, run, 0)
        if not to_slots:
            d0 = tail_off_ref[t]
            _run_pieces(tail_len_ref[t], lambda off, size: action(tail_copy(d0, off, size)))

    each_run(lambda cp: cp.start())
    each_run(lambda cp: cp.wait())


def _run_copy(src, src_off, dst_off, run_len, ends, padded, tail_off, tail_len, n_dst, to_slots, name):
    W = src.shape[1]
    steps = run_len.shape[0] // N_EXPERTS
    grid_spec = pltpu.PrefetchScalarGridSpec(
        num_scalar_prefetch=7,
        grid=(steps,),
        in_specs=[pl.BlockSpec(memory_space=pl.ANY)],
        out_specs=pl.BlockSpec(memory_space=pl.ANY),
        scratch_shapes=[pltpu.VMEM((EXPERT_TILE, W), U32), pltpu.SemaphoreType.DMA],
    )
    return pl.pallas_call(
        functools.partial(_run_copy_body, to_slots),
        grid_spec=grid_spec,
        out_shape=jax.ShapeDtypeStruct((n_dst, W), U32),
        compiler_params=pltpu.CompilerParams(dimension_semantics=("arbitrary",), vmem_limit_bytes=VMEM_LIMIT,
                                             has_side_effects=True),
        name=name,
    )(src_off, dst_off, run_len, ends, padded, tail_off, tail_len, src)


def _deinterleave_body(w_ref, o_ref):
    n = 2 * LANES
    src = lax.broadcasted_iota(I32, (n, n), 0)
    dst = lax.broadcasted_iota(I32, (n, n), 1)
    want = jnp.where(dst < LANES, 2 * dst, 2 * (dst - LANES) + 1)
    perm = (src == want).astype(BF16)
    for blk in range(w_ref.shape[2] // n):
        cols = slice(blk * n, (blk + 1) * n)
        o_ref[0, :, cols] = jnp.dot(w_ref[0, :, cols].astype(BF16), perm,
                                    preferred_element_type=F32).astype(BF16)


def _deinterleave_up(w_up):
    N, D, F2 = w_up.shape
    return pl.pallas_call(
        _deinterleave_body,
        grid=(N,),
        in_specs=[pl.BlockSpec((1, D, F2), lambda e: (e, 0, 0))],
        out_specs=pl.BlockSpec((1, D, F2), lambda e: (e, 0, 0)),
        out_shape=jax.ShapeDtypeStruct((N, D, F2), BF16),
        compiler_params=_params(("arbitrary",)),
        name="deinterleave_up",
    )(w_up)


def _experts_body(te_ref, nact_ref, xs_ref, wup_ref, bup_ref, wdn_ref, bdn_ref, ys_ref):
    i = pl.program_id(0)
    ff = wdn_ref.shape[1]

    @pl.when(i < nact_ref[0])
    def _():
        wd = xs_ref[...]
        x = jnp.concatenate([_unpack_hi(wd), _unpack_lo(wd)], axis=1).astype(BF16)
        hh = jnp.dot(x, wup_ref[0], preferred_element_type=F32) + bup_ref[0]
        acts = []
        for blk in range(ff // LANES):
            glu = jnp.minimum(hh[:, 2 * blk * LANES:(2 * blk + 1) * LANES], SWIGLU_LIMIT)
            lin = jnp.clip(hh[:, (2 * blk + 1) * LANES:(2 * blk + 2) * LANES], -SWIGLU_LIMIT, SWIGLU_LIMIT)
            acts.append((glu * _sigmoid(SWIGLU_ALPHA * glu) * (lin + 1.0)).astype(BF16))
        act = jnp.concatenate(acts, axis=1)
        y = jnp.dot(act, wdn_ref[0], preferred_element_type=F32) + bdn_ref[0]
        ys_ref[...] = _pack_bf16_pairs(y)

    @pl.when(i >= nact_ref[0])
    def _():
        ys_ref[...] = jnp.zeros_like(ys_ref)


def _experts(xs, tile_expert, n_active, w_up, b_up, w_dn, b_dn, layer):
    P, W = xs.shape
    tm = EXPERT_TILE
    D = 2 * W
    ff2 = w_up.shape[2]
    ff = w_dn.shape[1]
    ex = lambda i, te, na: (layer * N_EXPERTS + te[i], 0, 0)
    grid_spec = pltpu.PrefetchScalarGridSpec(
        num_scalar_prefetch=2,
        grid=(P // tm,),
        in_specs=[
            pl.BlockSpec((tm, W), lambda i, te, na: (jnp.where(i < na[0], i, 0), 0)),
            pl.BlockSpec((1, D, ff2), ex),
            pl.BlockSpec((1, 1, ff2), ex),
            pl.BlockSpec((1, ff, D), ex),
            pl.BlockSpec((1, 1, D), ex),
        ],
        out_specs=pl.BlockSpec((tm, W), lambda i, te, na: (i, 0)),
    )
    return pl.pallas_call(
        _experts_body,
        grid_spec=grid_spec,
        out_shape=jax.ShapeDtypeStruct((P, W), U32),
        compiler_params=_params(("arbitrary",)),
        name="experts",
    )(tile_expert, n_active, xs, w_up, b_up, w_dn, b_dn)


def _combine_body(alpha, x_ref, yc_ref, pos_ref, wts_ref, gm_ref, lng_ref, lnb_ref, o_ref):
    tt = x_ref.shape[1]
    n = yc_ref.shape[0]
    p_id = lax.broadcasted_iota(I32, (tt, n), 1)
    a = jnp.zeros((tt, n), F32)
    for k in reversed(range(TOP_K)):
        a = jnp.where(p_id == pos_ref[:, k:k + 1], wts_ref[:, k:k + 1], a)
    a = a.astype(BF16)
    wd = yc_ref[...]
    hi = jnp.dot(a, _unpack_hi(wd).astype(BF16), preferred_element_type=F32)
    lo = jnp.dot(a, _unpack_lo(wd).astype(BF16), preferred_element_type=F32)
    y = jnp.concatenate([hi, lo], axis=1)
    o_ref[0] = _layer_norm(alpha * x_ref[0] + (1.0 + gm_ref[0]) * y, lng_ref[0], lnb_ref[0])


def _combine(x1, yc, pos_t, wts_t, gm, ln_g, ln_b, layer, depth):
    B, S, D = x1.shape
    tt = TOKEN_TILE
    nt = S // tt
    return pl.pallas_call(
        functools.partial(_combine_body, (2.0 * depth) ** 0.25),
        grid=(B, nt),
        in_specs=[
            pl.BlockSpec((1, tt, D), lambda b, i: (b, i, 0)),
            pl.BlockSpec((RUN_TILE_ROWS, D // 2), lambda b, i: (b * nt + i, 0)),
            pl.BlockSpec((tt, TOP_K), lambda b, i: (b * nt + i, 0)),
            pl.BlockSpec((tt, TOP_K), lambda b, i: (b * nt + i, 0)),
            pl.BlockSpec((1, 1, D), lambda b, i: (b, 0, 0)),
            pl.BlockSpec((1, 1, D), lambda b, i: (layer, 0, 0)),
            pl.BlockSpec((1, 1, D), lambda b, i: (layer, 0, 0)),
        ],
        out_specs=pl.BlockSpec((1, tt, D), lambda b, i: (b, i, 0)),
        out_shape=jax.ShapeDtypeStruct((B, S, D), F32),
        compiler_params=_params(("arbitrary", "arbitrary")),
        name="combine",
    )(x1, yc, pos_t, wts_t, gm, ln_g, ln_b)


def _rope_tables(S):
    half = ROPE_DIM // 2
    inv = ROPE_THETA ** (-jnp.arange(half, dtype=F32) / half)
    ang = jnp.arange(S, dtype=F32)[:, None] * inv[None, :]
    cos, sin = jnp.cos(ang), jnp.sin(ang)
    lane = jnp.arange(LANES) % HEAD_DIM
    pick = lane % half
    cos_l, sin_l = cos[:, pick], sin[:, pick]
    cos_t = jnp.where(lane < ROPE_DIM, cos_l, 1.0)
    s_up = jnp.where((lane >= half) & (lane < ROPE_DIM), sin_l, 0.0)
    s_dn = jnp.where(lane < half, -sin_l, 0.0)
    return cos_t.astype(F32), s_up.astype(F32), s_dn.astype(F32)


def kernel(x, c, w_ada, b_ada, w_in, spatial_w, spatial_b, w_branch_a, w_branch_b, w_out, ln1_g, ln1_b,
           w_router, b_router, w_up, b_up, w_down, b_down, ln2_g, ln2_b):
    B, S, D = x.shape
    L = w_ada.shape[0]
    T = B * S
    gw = B_GROUP_WIDTH
    nq = len(DILATIONS) * gw
    assert S % ATT_BLOCK == 0 and S % TOKEN_TILE == 0 and D % (2 * LANES) == 0

    q0, k0, v0, g0 = 2 * D, 2 * D + nq, 2 * D + 2 * nq, 2 * D + 3 * nq
    w_a = jnp.concatenate([w_in[:, :, :2 * D], w_in[:, :, g0:g0 + D]], axis=2).astype(BF16)
    cols = []
    for g in range(len(DILATIONS)):
        for base in (q0, k0, v0):
            cols.append(w_in[:, :, base + g * gw:base + (g + 1) * gw])
    cols.append(w_in[:, :, g0 + D:g0 + 2 * D])
    w_c = jnp.concatenate(cols, axis=2).astype(BF16)
    w_ba = w_branch_a.astype(BF16)
    w_bb = w_branch_b.astype(BF16)
    w_o = w_out.astype(BF16)
    sp_b = spatial_b[..., None]
    w_rt = jnp.swapaxes(w_router, 1, 2)
    b_r = b_router[..., None]
    ff = w_down.shape[2]
    w_up_p = _deinterleave_up(w_up.reshape(L * N_EXPERTS, D, 2 * ff))
    b_up_p = jnp.swapaxes(b_up.reshape(L * N_EXPERTS, 1, ff // LANES, LANES, 2), 3, 4).reshape(
        L * N_EXPERTS, 1, 2 * ff)
    w_dn_p = w_down.astype(BF16).reshape(L * N_EXPERTS, ff, D)
    b_dn_p = b_down.reshape(L * N_EXPERTS, 1, D)
    ln1g, ln1b = ln1_g[:, None, :], ln1_b[:, None, :]
    ln2g, ln2b = ln2_g[:, None, :], ln2_b[:, None, :]
    cos_t, sup_t, sdn_t = _rope_tables(S)

    mod = _ada_mod(c, w_ada, b_ada)
    n_slots = T * TOP_K + (T // TOKEN_TILE) * N_EXPERTS * RUN_ALIGN + N_EXPERTS * EXPERT_TILE
    n_tiles = n_slots // EXPERT_TILE

    for l in range(L):
        sh1, sc1, g1, sh2, sc2, g2 = [mod[l, :, None, i * D:(i + 1) * D] for i in range(6)]
        ya = _branch_a(x, sc1, sh1, w_a, spatial_w, sp_b, w_ba, l)
        *qkv, gb = _qkv(x, sc1, sh1, w_c, cos_t, sup_t, sdn_t, l)
        ob = _attention(qkv)
        x1, hc, pos, wts, cnt = _post_mixer(x, ya, gb, ob, w_bb, w_o, g1, ln1g, ln1b, sc2, sh2, w_rt, b_r, l, L)
        n_tok_tiles = T // TOKEN_TILE
        tc = cnt.reshape(n_tok_tiles, N_EXPERTS, LANES)[:, :, 0].astype(I32)
        tc = ((tc + RUN_ALIGN - 1) // RUN_ALIGN) * RUN_ALIGN
        off_in_tile = jnp.cumsum(tc, axis=1) - tc
        seen_before = jnp.cumsum(tc, axis=0) - tc
        counts = jnp.sum(tc, axis=0)
        padded = ((counts + EXPERT_TILE - 1) // EXPERT_TILE) * EXPERT_TILE
        ends = jnp.cumsum(padded).astype(I32)
        starts = ends - padded
        tile_base = jnp.arange(n_tok_tiles, dtype=I32) * RUN_TILE_ROWS
        tok_off = (tile_base[:, None] + off_in_tile).astype(I32).reshape(-1)
        slot_off = (starts[None, :] + seen_before).astype(I32).reshape(-1)
        run_len = tc.reshape(-1)
        used = jnp.sum(tc, axis=1)
        tail_off = (tile_base + used).astype(I32)
        tail_len = (RUN_TILE_ROWS - used).astype(I32)
        n_active = (ends[-1] // EXPERT_TILE).astype(I32)
        tile_start = jnp.arange(n_tiles, dtype=I32) * EXPERT_TILE
        te = jnp.sum((ends[None, :] <= tile_start[:, None]).astype(I32), axis=1)
        te_last = jnp.sum((ends <= (n_active - 1) * EXPERT_TILE).astype(I32))
        te = jnp.minimum(jnp.where(tile_start < ends[-1], te, te_last), N_EXPERTS - 1).astype(I32)

        padded = padded.astype(I32)
        xs = _run_copy(hc, tok_off, slot_off, run_len, ends, padded, tail_off, tail_len, n_slots, True, "dispatch")
        ys = _experts(xs, te, n_active.reshape(1), w_up_p, b_up_p, w_dn_p, b_dn_p, l)
        yc = _run_copy(ys, slot_off, tok_off, run_len, ends, padded, tail_off, tail_len,
                       n_tok_tiles * RUN_TILE_ROWS, False, "undispatch")
        x = _combine(x1, yc, pos.T, wts.T, g2, ln2g, ln2b, l, L)
    return x
```

```python
import functools
import math

import jax
import jax.numpy as jnp
from jax import lax
from jax.experimental import pallas as pl
from jax.experimental.pallas import tpu as pltpu

BF16 = jnp.bfloat16
F32 = jnp.float32
U32 = jnp.uint32
I32 = jnp.int32

A_CHUNK = 128
A_GROUPS = 4
HEAD_DIM = 64
B_HEADS = 8
DILATIONS = (1, 4, 16)
ATT_SPAN = 128
B_GROUP_WIDTH = B_HEADS * HEAD_DIM
ROPE_DIM = HEAD_DIM // 4
ROPE_THETA = 500000.0
N_EXPERTS = 32
TOP_K = 4
SWIGLU_LIMIT = 7.0
SWIGLU_ALPHA = 1.702
LN_EPS = 1e-5

LANES = 128
VMEM_LIMIT = 56 * 1024 * 1024
TOKEN_TILE = 512
ATT_BLOCK = 128 * DILATIONS[-1]
EXPERT_TILE = 512
ATT_CHUNK = 4
RUN_BITS = TOKEN_TILE.bit_length()
RUN_ALIGN = 8
RUN_TILE_ROWS = TOP_K * TOKEN_TILE + N_EXPERTS * RUN_ALIGN
PERM_ROWS = 256


def _gelu(x):
    return 0.5 * x * (1.0 + jnp.tanh(math.sqrt(2.0 / math.pi) * (x + 0.044715 * (x * x * x))))


def _sigmoid(x):
    return 1.0 / (1.0 + jnp.exp(-x))


def _layer_norm(r, g, b):
    mu = jnp.mean(r, axis=-1, keepdims=True)
    c = r - mu
    var = jnp.mean(c * c, axis=-1, keepdims=True)
    return c * lax.rsqrt(var + LN_EPS) * g + b


def _pack_bf16_pairs(x):
    w = x.shape[1] // 2
    bits = lax.bitcast_convert_type(x.astype(BF16).astype(F32), U32)
    return bits[:, :w] | lax.shift_right_logical(bits[:, w:], jnp.uint32(16))


def _unpack_hi(wd):
    return lax.bitcast_convert_type(wd & jnp.uint32(0xFFFF0000), F32)


def _unpack_lo(wd):
    return lax.bitcast_convert_type(lax.shift_left(wd, jnp.uint32(16)), F32)


def _params(sem):
    return pltpu.CompilerParams(dimension_semantics=sem, vmem_limit_bytes=VMEM_LIMIT)


def _ada_body(c_ref, w_ref, b_ref, o_ref):
    o_ref[0] = jnp.dot(c_ref[...], w_ref[0], preferred_element_type=F32,
                       precision=lax.Precision.HIGHEST) + b_ref[0]


def _ada_mod(c, w_ada, b_ada):
    L, D, N = w_ada.shape
    B = c.shape[0]
    tn = D
    return pl.pallas_call(
        _ada_body,
        grid=(L, N // tn),
        in_specs=[
            pl.BlockSpec((B, D), lambda l, j: (0, 0)),
            pl.BlockSpec((1, D, tn), lambda l, j: (l, 0, j)),
            pl.BlockSpec((1, 1, tn), lambda l, j: (l, 0, j)),
        ],
        out_specs=pl.BlockSpec((1, B, tn), lambda l, j: (l, 0, j)),
        out_shape=jax.ShapeDtypeStruct((L, B, N), F32),
        compiler_params=_params(("arbitrary", "arbitrary")),
        name="ada_mod",
    )(c, w_ada, b_ada.reshape(L, 1, N))


def _branch_a_body(x_ref, sc_ref, sh_ref, wa_ref, spw_ref, spb_ref, wba_ref, o_ref, a_scr):
    tm, D = x_ref.shape[1], x_ref.shape[2]
    gd = D // A_GROUPS
    h = (x_ref[0] * (1.0 + sc_ref[0]) + sh_ref[0]).astype(BF16)
    row = lax.broadcasted_iota(I32, (A_CHUNK, A_CHUNK), 0)
    col = lax.broadcasted_iota(I32, (A_CHUNK, A_CHUNK), 1)
    causal = row >= col
    for g in range(A_GROUPS):
        v = _gelu(jnp.dot(h, wa_ref[0, :, D + g * gd:D + (g + 1) * gd], preferred_element_type=F32))
        mu = jnp.mean(v, axis=-1, keepdims=True)
        vc = v - mu
        var = jnp.mean(vc * vc, axis=-1, keepdims=True)
        vn = (vc * lax.rsqrt(var + LN_EPS)).astype(BF16)
        u = _gelu(jnp.dot(h, wa_ref[0, :, g * gd:(g + 1) * gd], preferred_element_type=F32))
        ws = jnp.where(causal, spw_ref[0, g], 0.0).astype(BF16)
        bias = spb_ref[0, g]
        for c in range(tm // A_CHUNK):
            rows = slice(c * A_CHUNK, (c + 1) * A_CHUNK)
            sv = jnp.dot(ws, vn[rows], preferred_element_type=F32) + bias
            a_scr[rows, g * gd:(g + 1) * gd] = (u[rows] * sv).astype(BF16)
    ya = jnp.dot(a_scr[...], wba_ref[0], preferred_element_type=F32)
    gate = _sigmoid(jnp.dot(h, wa_ref[0, :, 2 * D:3 * D], preferred_element_type=F32))
    o_ref[0] = (gate * ya).astype(BF16)


def _branch_a(x, sc, sh, w_a, sp_w, sp_b, w_ba, layer):
    B, S, D = x.shape
    tm = TOKEN_TILE
    return pl.pallas_call(
        _branch_a_body,
        grid=(B, S // tm),
        in_specs=[
            pl.BlockSpec((1, tm, D), lambda b, i: (b, i, 0)),
            pl.BlockSpec((1, 1, D), lambda b, i: (b, 0, 0)),
            pl.BlockSpec((1, 1, D), lambda b, i: (b, 0, 0)),
            pl.BlockSpec((1, D, 3 * D), lambda b, i: (layer, 0, 0)),
            pl.BlockSpec((1, A_GROUPS, A_CHUNK, A_CHUNK), lambda b, i: (layer, 0, 0, 0)),
            pl.BlockSpec((1, A_GROUPS, A_CHUNK, 1), lambda b, i: (layer, 0, 0, 0)),
            pl.BlockSpec((1, D, D), lambda b, i: (layer, 0, 0)),
        ],
        out_specs=pl.BlockSpec((1, tm, D), lambda b, i: (b, i, 0)),
        out_shape=jax.ShapeDtypeStruct((B, S, D), BF16),
        scratch_shapes=[pltpu.VMEM((tm, D), BF16)],
        compiler_params=_params(("arbitrary", "arbitrary")),
        name="branch_a",
    )(x, sc, sh, w_a, sp_w, sp_b, w_ba)


def _rope(z, cos, s_up, s_dn):
    half = ROPE_DIM // 2
    parts = []
    for c in range(z.shape[1] // LANES):
        zc = z[:, c * LANES:(c + 1) * LANES]
        parts.append(zc * cos + pltpu.roll(zc, half, 1) * s_up + pltpu.roll(zc, LANES - half, 1) * s_dn)
    return jnp.concatenate(parts, axis=1)


def _qkv_body(x_ref, sc_ref, sh_ref, w_ref, cos_ref, sup_ref, sdn_ref, *rest):
    outs = rest[:9]
    gate_ref = rest[9]
    scr = rest[10]
    tm, D = x_ref.shape[1], x_ref.shape[2]
    gw = B_GROUP_WIDTH
    h = (x_ref[0] * (1.0 + sc_ref[0]) + sh_ref[0]).astype(BF16)
    cos, s_up, s_dn = cos_ref[...], sup_ref[...], sdn_ref[...]
    for g, d in enumerate(DILATIONS):
        for part in range(3):
            col = (3 * g + part) * gw
            z = jnp.dot(h, w_ref[0, :, col:col + gw], preferred_element_type=F32)
            if part < 2:
                z = _rope(z, cos, s_up, s_dn)
            if part == 0:
                z = z * (HEAD_DIM ** -0.5)
            o_ref = outs[3 * g + part]
            if d == 1:
                o_ref[0, 0] = z.astype(BF16)
            else:
                nc = gw // LANES
                for c in range(nc):
                    scr[c] = z[:, c * LANES:(c + 1) * LANES]
                for r in range(d):
                    o_ref[0, r] = jnp.concatenate(
                        [scr[c, pl.ds(r, tm // d, stride=d), :] for c in range(nc)], axis=1).astype(BF16)
    gate = _sigmoid(jnp.dot(h, w_ref[0, :, 9 * gw:9 * gw + D], preferred_element_type=F32))
    gate_ref[0] = gate.astype(BF16)


def _qkv(x, sc, sh, w_c, cos_t, sup_t, sdn_t, layer):
    B, S, D = x.shape
    tm = TOKEN_TILE
    gw = B_GROUP_WIDTH
    nw = w_c.shape[2]
    out_shapes, out_specs = [], []
    for d in DILATIONS:
        for _ in range(3):
            out_shapes.append(jax.ShapeDtypeStruct((B, d, S // d, gw), BF16))
            out_specs.append(pl.BlockSpec((1, d, tm // d, gw), lambda b, i: (b, 0, i, 0)))
    out_shapes.append(jax.ShapeDtypeStruct((B, S, D), BF16))
    out_specs.append(pl.BlockSpec((1, tm, D), lambda b, i: (b, i, 0)))
    return pl.pallas_call(
        _qkv_body,
        grid=(B, S // tm),
        in_specs=[
            pl.BlockSpec((1, tm, D), lambda b, i: (b, i, 0)),
            pl.BlockSpec((1, 1, D), lambda b, i: (b, 0, 0)),
            pl.BlockSpec((1, 1, D), lambda b, i: (b, 0, 0)),
            pl.BlockSpec((1, D, nw), lambda b, i: (layer, 0, 0), pipeline_mode=pl.Buffered(1)),
            pl.BlockSpec((tm, LANES), lambda b, i: (i, 0)),
            pl.BlockSpec((tm, LANES), lambda b, i: (i, 0)),
            pl.BlockSpec((tm, LANES), lambda b, i: (i, 0)),
        ],
        out_specs=out_specs,
        out_shape=out_shapes,
        scratch_shapes=[pltpu.VMEM((gw // LANES, tm, LANES), F32)],
        compiler_params=_params(("arbitrary", "arbitrary")),
        name="qkv",
    )(x, sc, sh, w_c, cos_t, sup_t, sdn_t)


def _attention_body(*refs):
    ins = refs[:15]
    o_ref = refs[15]
    o_scr, m_scr, l_scr, bias_scr, p_scr = refs[16:21]
    n = pl.program_id(1)
    blk = ATT_SPAN
    qi = lax.broadcasted_iota(I32, (blk, 2 * blk), 0)
    kj = lax.broadcasted_iota(I32, (blk, 2 * blk), 1)
    band = (kj >= qi) & (kj <= qi + ATT_SPAN)
    neg = jnp.float32(-jnp.inf)
    bias_scr[0] = jnp.where(band, 0.0, neg)
    bias_scr[1] = jnp.where(band & ((kj >= blk) | (n > 0)), 0.0, neg)
    lane = lax.broadcasted_iota(I32, (blk, LANES), 1)
    head0 = lane < HEAD_DIM
    ones = jnp.ones((2 * blk, LANES), BF16)

    def store(scr, g, d, r, j, val):
        view = scr.at[g, pl.ds(j * blk * d, blk * d)]
        if d == 1:
            view[...] = val
        else:
            view[pl.ds(r, blk, stride=d), :] = val

    def window(prev_ref, cur_ref, r, j):
        if j == 0:
            return jnp.concatenate([prev_ref[0, r], cur_ref[0, r, 0:blk]], axis=0)
        return cur_ref[0, r, (j - 1) * blk:(j + 1) * blk]

    chunk_id = 0
    for g, d in enumerate(DILATIONS):
        q_ref, kc_ref, kp_ref, vc_ref, vp_ref = ins[5 * g:5 * g + 5]
        nj = ATT_BLOCK // d // blk
        units = [(r, j) for r in range(d) for j in range(nj)]
        for c0 in range(0, len(units), ATT_CHUNK):
            chunk = units[c0:c0 + ATT_CHUNK]
            pbase = (chunk_id % 2) * 2 * ATT_CHUNK
            chunk_id += 1
            for ui, (r, j) in enumerate(chunk):
                q = q_ref[0, r, j * blk:(j + 1) * blk]
                kk = window(kp_ref, kc_ref, r, j)
                bias = bias_scr[1 if j == 0 else 0]
                zero = jnp.zeros_like(q)
                ms = []
                for h, qh in enumerate((jnp.where(head0, q, zero), jnp.where(head0, zero, q))):
                    s = lax.dot_general(qh, kk, (((1,), (1,)), ((), ())), preferred_element_type=F32) + bias
                    m = jnp.max(s, axis=1, keepdims=True)
                    p_scr[pbase + 2 * ui + h] = jnp.exp(s - m).astype(BF16)
                    ms.append(m)
                store(m_scr, g, d, r, j, jnp.where(head0, ms[0], ms[1]))
            for ui, (r, j) in enumerate(chunk):
                ve = jnp.concatenate([window(vp_ref, vc_ref, r, j), ones], axis=1)
                oe = [jnp.dot(p_scr[pbase + 2 * ui + h], ve, preferred_element_type=F32) for h in range(2)]
                store(o_scr, g, d, r, j, jnp.where(head0, oe[0][:, :LANES], oe[1][:, :LANES]))
                store(l_scr, g, d, r, j, jnp.where(head0, oe[0][:, LANES:], oe[1][:, LANES:]))

    def merge(c, carry):
        rows = pl.ds(pl.multiple_of(c * blk, blk), blk)
        ms = [m_scr[g, rows] for g in range(3)]
        m_ref = jnp.maximum(jnp.maximum(ms[0], ms[1]), ms[2])
        num = jnp.zeros((blk, LANES), F32)
        den = jnp.zeros((blk, LANES), F32)
        for g in range(3):
            w = jnp.exp(ms[g] - m_ref)
            num = num + w * o_scr[g, rows]
            den = den + w * l_scr[g, rows]
        o_ref[0, rows] = (num / den).astype(BF16)
        return carry
    lax.fori_loop(0, ATT_BLOCK // blk, merge, 0)


def _attention(qkv):
    B = qkv[0].shape[0]
    S = qkv[0].shape[2]
    gw = B_GROUP_WIDTH
    blk = ATT_SPAN
    nblk = S // ATT_BLOCK
    in_specs, args = [], []
    for g, d in enumerate(DILATIONS):
        rows = ATT_BLOCK // d
        per = rows // blk
        cur = pl.BlockSpec((1, d, rows, LANES), lambda b, n, hp: (b, 0, n, hp))
        prev = pl.BlockSpec((1, d, blk, LANES),
                            lambda b, n, hp, per=per: (b, 0, jnp.maximum(n * per - 1, 0), hp))
        q, k, v = qkv[3 * g:3 * g + 3]
        in_specs += [cur, cur, prev, cur, prev]
        args += [q, k, k, v, v]
    return pl.pallas_call(
        _attention_body,
        grid=(B, nblk, gw // LANES),
        in_specs=in_specs,
        out_specs=pl.BlockSpec((1, ATT_BLOCK, LANES), lambda b, n, hp: (b, n, hp)),
        out_shape=jax.ShapeDtypeStruct((B, S, gw), BF16),
        scratch_shapes=[pltpu.VMEM((3, ATT_BLOCK, LANES), F32)] * 3 + [
            pltpu.VMEM((2, blk, 2 * blk), F32),
            pltpu.VMEM((4 * ATT_CHUNK, blk, 2 * blk), BF16)],
        compiler_params=_params(("arbitrary", "arbitrary", "arbitrary")),
        name="attention",
    )(*args)


def _post_mixer_body(alpha, x_ref, ya_ref, gb_ref, ob_ref, wbb_ref, wout_ref, gm_ref, lng_ref, lnb_ref,
                     sc_ref, sh_ref, wr_ref, br_ref,
                     x1_ref, hc_ref, pos_ref, wts_ref, cnt_ref):
    tt = x_ref.shape[1]
    yb = jnp.dot(ob_ref[0], wbb_ref[0], preferred_element_type=F32)
    merged = (ya_ref[0].astype(F32) + gb_ref[0].astype(F32) * yb).astype(BF16)
    y = jnp.dot(merged, wout_ref[0], preferred_element_type=F32)
    x1 = _layer_norm(alpha * x_ref[0] + (1.0 + gm_ref[0]) * y, lng_ref[0], lnb_ref[0])
    x1_ref[0] = x1
    h2 = x1 * (1.0 + sc_ref[0]) + sh_ref[0]

    logits = lax.dot_general(wr_ref[0], h2, (((1,), (1,)), ((), ())), preferred_element_type=F32,
                             precision=lax.Precision.HIGHEST) + br_ref[0]
    eid = lax.broadcasted_iota(I32, (N_EXPERTS, tt), 0)
    vals, hots = [], []
    cur = logits
    for k in range(TOP_K):
        m = jnp.max(cur, axis=0, keepdims=True)
        sel = jnp.min(jnp.where(cur == m, eid, N_EXPERTS), axis=0, keepdims=True)
        hot = eid == sel
        vals.append(m)
        hots.append(hot)
        cur = jnp.where(hot, -jnp.inf, cur)
    ex = [jnp.exp(v - vals[0]) for v in vals]
    tot = ex[0] + ex[1] + ex[2] + ex[3]
    for k in range(TOP_K):
        wts_ref[k:k + 1, :] = ex[k] / tot

    member = jnp.zeros((N_EXPERTS, tt), F32)
    for hot in hots:
        member = member + hot.astype(F32)
    before = (lax.broadcasted_iota(I32, (tt, tt), 0) < lax.broadcasted_iota(I32, (tt, tt), 1))
    excl = jnp.dot(member.astype(BF16), before.astype(BF16), preferred_element_type=F32)
    cnt = jnp.broadcast_to(jnp.sum(member, axis=1, keepdims=True), (N_EXPERTS, LANES))
    cnt_ref[...] = cnt
    lower = (lax.broadcasted_iota(I32, (N_EXPERTS, N_EXPERTS), 1)
             < lax.broadcasted_iota(I32, (N_EXPERTS, N_EXPERTS), 0)).astype(F32)
    cnt_al = jnp.ceil(cnt * (1.0 / RUN_ALIGN)) * RUN_ALIGN
    off = jnp.dot(lower, cnt_al, preferred_element_type=F32, precision=lax.Precision.HIGHEST)[:, 0:1]
    base = off + excl
    pos = []
    for k in range(TOP_K):
        pk = jnp.sum(jnp.where(hots[k], base, 0.0), axis=0, keepdims=True).astype(I32)
        pos_ref[k:k + 1, :] = pk
        pos.append(pk)

    h2b = h2.astype(BF16)
    for c in range(hc_ref.shape[0] // PERM_ROWS):
        p_id = lax.broadcasted_iota(I32, (PERM_ROWS, tt), 0) + c * PERM_ROWS
        sel = (p_id == pos[0]) | (p_id == pos[1]) | (p_id == pos[2]) | (p_id == pos[3])
        rows = jnp.dot(jnp.where(sel, 1.0, 0.0).astype(BF16), h2b, preferred_element_type=F32)
        hc_ref[c * PERM_ROWS:(c + 1) * PERM_ROWS, :] = _pack_bf16_pairs(rows)


def _post_mixer(x, ya, gb, ob, w_bb, w_out, gm, ln_g, ln_b, sc, sh, w_rt, b_r, layer, depth):
    B, S, D = x.shape
    tt = TOKEN_TILE
    nt = S // tt
    T = B * S
    gw = B_GROUP_WIDTH
    tok = lambda b, i: (b, i, 0)
    per_b = lambda b, i: (b, 0, 0)
    per_l3 = lambda b, i: (layer, 0, 0)
    flat = lambda b, i: (0, b * nt + i)
    return pl.pallas_call(
        functools.partial(_post_mixer_body, (2.0 * depth) ** 0.25),
        grid=(B, nt),
        in_specs=[
            pl.BlockSpec((1, tt, D), tok),
            pl.BlockSpec((1, tt, D), tok),
            pl.BlockSpec((1, tt, D), tok),
            pl.BlockSpec((1, tt, gw), tok),
            pl.BlockSpec((1, gw, D), per_l3),
            pl.BlockSpec((1, D, D), per_l3),
            pl.BlockSpec((1, 1, D), per_b),
            pl.BlockSpec((1, 1, D), per_l3),
            pl.BlockSpec((1, 1, D), per_l3),
            pl.BlockSpec((1, 1, D), per_b),
            pl.BlockSpec((1, 1, D), per_b),
            pl.BlockSpec((1, N_EXPERTS, D), per_l3),
            pl.BlockSpec((1, N_EXPERTS, 1), per_l3),
        ],
        out_specs=[
            pl.BlockSpec((1, tt, D), tok),
            pl.BlockSpec((RUN_TILE_ROWS, D // 2), lambda b, i: (b * nt + i, 0)),
            pl.BlockSpec((TOP_K, tt), flat),
            pl.BlockSpec((TOP_K, tt), flat),
            pl.BlockSpec((N_EXPERTS, LANES), lambda b, i: (b * nt + i, 0)),
        ],
        out_shape=[
            jax.ShapeDtypeStruct((B, S, D), F32),
            jax.ShapeDtypeStruct((B * nt * RUN_TILE_ROWS, D // 2), U32),
            jax.ShapeDtypeStruct((TOP_K, T), I32),
            jax.ShapeDtypeStruct((TOP_K, T), F32),
            jax.ShapeDtypeStruct((B * nt * N_EXPERTS, LANES), F32),
        ],
        compiler_params=_params(("arbitrary", "arbitrary")),
        name="post_mixer",
    )(x, ya, gb, ob, w_bb, w_out, gm, ln_g, ln_b, sc, sh, w_rt, b_r)


def _run_pieces(n, fn):
    for bit in reversed(range(RUN_ALIGN.bit_length() - 1, RUN_BITS)):
        size = 1 << bit
        done = pl.multiple_of((n >> (bit + 1)) << (bit + 1), RUN_ALIGN)

        @pl.when((n & size) != 0)
        def _():
            fn(done, size)


def _each_run(step, len_ref, a_off_ref, b_off_ref, make_copy, action):
    def run(e, carry):
        i = step * N_EXPERTS + e
        a0, b0 = a_off_ref[i], b_off_ref[i]
        _run_pieces(len_ref[i], lambda off, size: action(make_copy(
            pl.multiple_of(a0 + off, RUN_ALIGN), pl.multiple_of(b0 + off, RUN_ALIGN), size)))
        return carry
    lax.fori_loop(0, N_EXPERTS, run, 0)


def _dispatch_body(tile_off_ref, slot_off_ref, len_ref, ends_ref, padded_ref, hc_ref, xs_ref, zbuf, sem):
    t = pl.program_id(0)
    tm = zbuf.shape[0]

    @pl.when(t == 0)
    def _():
        zbuf[...] = jnp.zeros_like(zbuf)
        for e in range(N_EXPERTS):
            @pl.when(padded_ref[e] > 0)
            def _():
                start = pl.multiple_of(ends_ref[e] - tm, tm)
                cp = pltpu.make_async_copy(zbuf, xs_ref.at[pl.ds(start, tm)], sem)
                cp.start()
                cp.wait()

        def fill_unused(i, carry):
            cp = pltpu.make_async_copy(zbuf, xs_ref.at[pl.ds(pl.multiple_of(i * tm, tm), tm)], sem)
            cp.start()
            cp.wait()
            return carry
        lax.fori_loop(ends_ref[N_EXPERTS - 1] // tm, xs_ref.shape[0] // tm, fill_unused, 0)

    def make_copy(tile_row, slot_row, size):
        return pltpu.make_async_copy(hc_ref.at[pl.ds(tile_row, size)], xs_ref.at[pl.ds(slot_row, size)], sem)

    _each_run(t, len_ref, tile_off_ref, slot_off_ref, make_copy, lambda cp: cp.start())
    _each_run(t, len_ref, tile_off_ref, slot_off_ref, make_copy, lambda cp: cp.wait())


def _dispatch(hc, tile_off, slot_off, run_len, ends, padded, n_slots):
    W = hc.shape[1]
    grid_spec = pltpu.PrefetchScalarGridSpec(
        num_scalar_prefetch=5,
        grid=(hc.shape[0] // RUN_TILE_ROWS,),
        in_specs=[pl.BlockSpec((RUN_TILE_ROWS, W), lambda t, *_: (t, 0))],
        out_specs=pl.BlockSpec(memory_space=pl.ANY),
        scratch_shapes=[pltpu.VMEM((EXPERT_TILE, W), U32), pltpu.SemaphoreType.DMA],
    )
    return pl.pallas_call(
        _dispatch_body,
        grid_spec=grid_spec,
        out_shape=jax.ShapeDtypeStruct((n_slots, W), U32),
        compiler_params=pltpu.CompilerParams(dimension_semantics=("arbitrary",), vmem_limit_bytes=VMEM_LIMIT,
                                             has_side_effects=True),
        name="dispatch",
    )(tile_off, slot_off, run_len, ends, padded, hc)


def _deinterleave_body(w_ref, o_ref):
    n = 2 * LANES
    src = lax.broadcasted_iota(I32, (n, n), 0)
    dst = lax.broadcasted_iota(I32, (n, n), 1)
    want = jnp.where(dst < LANES, 2 * dst, 2 * (dst - LANES) + 1)
    perm = (src == want).astype(BF16)
    for blk in range(w_ref.shape[2] // n):
        cols = slice(blk * n, (blk + 1) * n)
        o_ref[0, :, cols] = jnp.dot(w_ref[0, :, cols].astype(BF16), perm,
                                    preferred_element_type=F32).astype(BF16)


def _deinterleave_up(w_up):
    N, D, F2 = w_up.shape
    return pl.pallas_call(
        _deinterleave_body,
        grid=(N,),
        in_specs=[pl.BlockSpec((1, D, F2), lambda e: (e, 0, 0))],
        out_specs=pl.BlockSpec((1, D, F2), lambda e: (e, 0, 0)),
        out_shape=jax.ShapeDtypeStruct((N, D, F2), BF16),
        compiler_params=_params(("arbitrary",)),
        name="deinterleave_up",
    )(w_up)


def _experts_body(te_ref, nact_ref, xs_ref, wup_ref, bup_ref, wdn_ref, bdn_ref, ys_ref):
    i = pl.program_id(0)
    ff = wdn_ref.shape[1]

    @pl.when(i < nact_ref[0])
    def _():
        wd = xs_ref[...]
        x = jnp.concatenate([_unpack_hi(wd), _unpack_lo(wd)], axis=1).astype(BF16)
        hh = jnp.dot(x, wup_ref[0], preferred_element_type=F32) + bup_ref[0]
        acts = []
        for blk in range(ff // LANES):
            glu = jnp.minimum(hh[:, 2 * blk * LANES:(2 * blk + 1) * LANES], SWIGLU_LIMIT)
            lin = jnp.clip(hh[:, (2 * blk + 1) * LANES:(2 * blk + 2) * LANES], -SWIGLU_LIMIT, SWIGLU_LIMIT)
            acts.append((glu * _sigmoid(SWIGLU_ALPHA * glu) * (lin + 1.0)).astype(BF16))
        act = jnp.concatenate(acts, axis=1)
        y = jnp.dot(act, wdn_ref[0], preferred_element_type=F32) + bdn_ref[0]
        ys_ref[...] = _pack_bf16_pairs(y)

    @pl.when(i >= nact_ref[0])
    def _():
        ys_ref[...] = jnp.zeros_like(ys_ref)


def _experts(xs, tile_expert, n_active, w_up, b_up, w_dn, b_dn, layer):
    P, W = xs.shape
    tm = EXPERT_TILE
    D = 2 * W
    ff2 = w_up.shape[2]
    ff = w_dn.shape[1]
    ex = lambda i, te, na: (layer * N_EXPERTS + te[i], 0, 0)
    grid_spec = pltpu.PrefetchScalarGridSpec(
        num_scalar_prefetch=2,
        grid=(P // tm,),
        in_specs=[
            pl.BlockSpec((tm, W), lambda i, te, na: (jnp.where(i < na[0], i, 0), 0)),
            pl.BlockSpec((1, D, ff2), ex),
            pl.BlockSpec((1, 1, ff2), ex),
            pl.BlockSpec((1, ff, D), ex),
            pl.BlockSpec((1, 1, D), ex),
        ],
        out_specs=pl.BlockSpec((tm, W), lambda i, te, na: (i, 0)),
    )
    return pl.pallas_call(
        _experts_body,
        grid_spec=grid_spec,
        out_shape=jax.ShapeDtypeStruct((P, W), U32),
        compiler_params=_params(("arbitrary",)),
        name="experts",
    )(tile_expert, n_active, xs, w_up, b_up, w_dn, b_dn)


def _combine_body(alpha, tile_off_ref, slot_off_ref, len_ref, x_ref, ys_ref, pos_ref, wts_ref, gm_ref, lng_ref,
                  lnb_ref, o_ref, ybuf, sem):
    tt = x_ref.shape[1]
    n = ybuf.shape[1]
    step = pl.program_id(0) * pl.num_programs(1) + pl.program_id(1)
    n_steps = pl.num_programs(0) * pl.num_programs(1)

    def runs(s, action):
        buf = s % 2

        def make_copy(tile_row, slot_row, size):
            return pltpu.make_async_copy(ys_ref.at[pl.ds(slot_row, size)], ybuf.at[buf, pl.ds(tile_row, size)],
                                         sem.at[buf])
        _each_run(s, len_ref, tile_off_ref, slot_off_ref, make_copy, action)

    def fetch(s):
        ybuf[s % 2, TOP_K * tt:n, :] = jnp.zeros((n - TOP_K * tt, ybuf.shape[2]), U32)
        runs(s, lambda cp: cp.start())

    @pl.when(step == 0)
    def _():
        fetch(step)

    @pl.when(step + 1 < n_steps)
    def _():
        fetch(step + 1)

    runs(step, lambda cp: cp.wait())

    p_id = lax.broadcasted_iota(I32, (tt, n), 1)
    a = jnp.zeros((tt, n), F32)
    for k in reversed(range(TOP_K)):
        a = jnp.where(p_id == pos_ref[:, k:k + 1], wts_ref[:, k:k + 1], a)
    a = a.astype(BF16)
    wd = ybuf[step % 2]
    hi = jnp.dot(a, _unpack_hi(wd).astype(BF16), preferred_element_type=F32)
    lo = jnp.dot(a, _unpack_lo(wd).astype(BF16), preferred_element_type=F32)
    y = jnp.concatenate([hi, lo], axis=1)
    o_ref[0] = _layer_norm(alpha * x_ref[0] + (1.0 + gm_ref[0]) * y, lng_ref[0], lnb_ref[0])


def _combine(x1, ys, tile_off, slot_off, run_len, pos_t, wts_t, gm, ln_g, ln_b, layer, depth):
    B, S, D = x1.shape
    tt = TOKEN_TILE
    nt = S // tt
    grid_spec = pltpu.PrefetchScalarGridSpec(
        num_scalar_prefetch=3,
        grid=(B, nt),
        in_specs=[
            pl.BlockSpec((1, tt, D), lambda b, i, *_: (b, i, 0)),
            pl.BlockSpec(memory_space=pl.ANY),
            pl.BlockSpec((tt, TOP_K), lambda b, i, *_: (b * nt + i, 0)),
            pl.BlockSpec((tt, TOP_K), lambda b, i, *_: (b * nt + i, 0)),
            pl.BlockSpec((1, 1, D), lambda b, i, *_: (b, 0, 0)),
            pl.BlockSpec((1, 1, D), lambda b, i, *_: (layer, 0, 0)),
            pl.BlockSpec((1, 1, D), lambda b, i, *_: (layer, 0, 0)),
        ],
        out_specs=pl.BlockSpec((1, tt, D), lambda b, i, *_: (b, i, 0)),
        scratch_shapes=[pltpu.VMEM((2, RUN_TILE_ROWS, D // 2), U32), pltpu.SemaphoreType.DMA((2,))],
    )
    return pl.pallas_call(
        functools.partial(_combine_body, (2.0 * depth) ** 0.25),
        grid_spec=grid_spec,
        out_shape=jax.ShapeDtypeStruct((B, S, D), F32),
        compiler_params=_params(("arbitrary", "arbitrary")),
        name="combine",
    )(tile_off, slot_off, run_len, x1, ys, pos_t, wts_t, gm, ln_g, ln_b)


def _rope_tables(S):
    half = ROPE_DIM // 2
    inv = ROPE_THETA ** (-jnp.arange(half, dtype=F32) / half)
    ang = jnp.arange(S, dtype=F32)[:, None] * inv[None, :]
    cos, sin = jnp.cos(ang), jnp.sin(ang)
    lane = jnp.arange(LANES) % HEAD_DIM
    pick = lane % half
    cos_l, sin_l = cos[:, pick], sin[:, pick]
    cos_t = jnp.where(lane < ROPE_DIM, cos_l, 1.0)
    s_up = jnp.where((lane >= half) & (lane < ROPE_DIM), sin_l, 0.0)
    s_dn = jnp.where(lane < half, -sin_l, 0.0)
    return cos_t.astype(F32), s_up.astype(F32), s_dn.astype(F32)


def kernel(x, c, w_ada, b_ada, w_in, spatial_w, spatial_b, w_branch_a, w_branch_b, w_out, ln1_g, ln1_b,
           w_router, b_router, w_up, b_up, w_down, b_down, ln2_g, ln2_b):
    B, S, D = x.shape
    L = w_ada.shape[0]
    T = B * S
    gw = B_GROUP_WIDTH
    nq = len(DILATIONS) * gw
    assert S % ATT_BLOCK == 0 and S % TOKEN_TILE == 0 and D % (2 * LANES) == 0

    q0, k0, v0, g0 = 2 * D, 2 * D + nq, 2 * D + 2 * nq, 2 * D + 3 * nq
    w_a = jnp.concatenate([w_in[:, :, :2 * D], w_in[:, :, g0:g0 + D]], axis=2).astype(BF16)
    cols = []
    for g in range(len(DILATIONS)):
        for base in (q0, k0, v0):
            cols.append(w_in[:, :, base + g * gw:base + (g + 1) * gw])
    cols.append(w_in[:, :, g0 + D:g0 + 2 * D])
    w_c = jnp.concatenate(cols, axis=2).astype(BF16)
    w_ba = w_branch_a.astype(BF16)
    w_bb = w_branch_b.astype(BF16)
    w_o = w_out.astype(BF16)
    sp_b = spatial_b[..., None]
    w_rt = jnp.swapaxes(w_router, 1, 2)
    b_r = b_router[..., None]
    ff = w_down.shape[2]
    w_up_p = _deinterleave_up(w_up.reshape(L * N_EXPERTS, D, 2 * ff))
    b_up_p = jnp.swapaxes(b_up.reshape(L * N_EXPERTS, 1, ff // LANES, LANES, 2), 3, 4).reshape(
        L * N_EXPERTS, 1, 2 * ff)
    w_dn_p = w_down.astype(BF16).reshape(L * N_EXPERTS, ff, D)
    b_dn_p = b_down.reshape(L * N_EXPERTS, 1, D)
    ln1g, ln1b = ln1_g[:, None, :], ln1_b[:, None, :]
    ln2g, ln2b = ln2_g[:, None, :], ln2_b[:, None, :]
    cos_t, sup_t, sdn_t = _rope_tables(S)

    mod = _ada_mod(c, w_ada, b_ada)
    n_slots = T * TOP_K + (T // TOKEN_TILE) * N_EXPERTS * RUN_ALIGN + N_EXPERTS * EXPERT_TILE
    n_tiles = n_slots // EXPERT_TILE

    for l in range(L):
        sh1, sc1, g1, sh2, sc2, g2 = [mod[l, :, None, i * D:(i + 1) * D] for i in range(6)]
        ya = _branch_a(x, sc1, sh1, w_a, spatial_w, sp_b, w_ba, l)
        *qkv, gb = _qkv(x, sc1, sh1, w_c, cos_t, sup_t, sdn_t, l)
        ob = _attention(qkv)
        x1, hc, pos, wts, cnt = _post_mixer(x, ya, gb, ob, w_bb, w_o, g1, ln1g, ln1b, sc2, sh2, w_rt, b_r, l, L)
        n_tok_tiles = T // TOKEN_TILE
        tc = cnt.reshape(n_tok_tiles, N_EXPERTS, LANES)[:, :, 0].astype(I32)
        tc = ((tc + RUN_ALIGN - 1) // RUN_ALIGN) * RUN_ALIGN
        off_in_tile = jnp.cumsum(tc, axis=1) - tc
        seen_before = jnp.cumsum(tc, axis=0) - tc
        counts = jnp.sum(tc, axis=0)
        padded = ((counts + EXPERT_TILE - 1) // EXPERT_TILE) * EXPERT_TILE
        ends = jnp.cumsum(padded).astype(I32)
        starts = ends - padded
        tile_off = off_in_tile.astype(I32).reshape(-1)
        slot_off = (starts[None, :] + seen_before).astype(I32).reshape(-1)
        run_len = tc.reshape(-1)
        n_active = (ends[-1] // EXPERT_TILE).astype(I32)
        tile_start = jnp.arange(n_tiles, dtype=I32) * EXPERT_TILE
        te = jnp.sum((ends[None, :] <= tile_start[:, None]).astype(I32), axis=1)
        te_last = jnp.sum((ends <= (n_active - 1) * EXPERT_TILE).astype(I32))
        te = jnp.minimum(jnp.where(tile_start < ends[-1], te, te_last), N_EXPERTS - 1).astype(I32)

        padded = padded.astype(I32)
        xs = _dispatch(hc, tile_off, slot_off, run_len, ends, padded, n_slots)
        ys = _experts(xs, te, n_active.reshape(1), w_up_p, b_up_p, w_dn_p, b_dn_p, l)
        x = _combine(x1, ys, tile_off, slot_off, run_len, pos.T, wts.T, g2, ln2g, ln2b, l, L)
    return x
```

```python
import functools
import math

import jax
import jax.numpy as jnp
from jax import lax
from jax.experimental import pallas as pl
from jax.experimental.pallas import tpu as pltpu

BF16 = jnp.bfloat16
F32 = jnp.float32
U32 = jnp.uint32
I32 = jnp.int32

A_CHUNK = 128
A_GROUPS = 4
HEAD_DIM = 64
B_HEADS = 8
DILATIONS = (1, 4, 16)
ATT_SPAN = 128
B_GROUP_WIDTH = B_HEADS * HEAD_DIM
ROPE_DIM = HEAD_DIM // 4
ROPE_THETA = 500000.0
N_EXPERTS = 32
TOP_K = 4
SWIGLU_LIMIT = 7.0
SWIGLU_ALPHA = 1.702
LN_EPS = 1e-5

LANES = 128
VMEM_LIMIT = 56 * 1024 * 1024
TOKEN_TILE = 512
ATT_BLOCK = 128 * DILATIONS[-1]
EXPERT_TILE = 512
ATT_CHUNK = 4
RUN_BITS = TOKEN_TILE.bit_length()
RUN_ALIGN = 8
RUN_TILE_ROWS = TOP_K * TOKEN_TILE + N_EXPERTS * RUN_ALIGN
PERM_ROWS = 256


def _gelu(x):
    return 0.5 * x * (1.0 + jnp.tanh(math.sqrt(2.0 / math.pi) * (x + 0.044715 * (x * x * x))))


def _sigmoid(x):
    return 1.0 / (1.0 + jnp.exp(-x))


def _layer_norm(r, g, b):
    mu = jnp.mean(r, axis=-1, keepdims=True)
    c = r - mu
    var = jnp.mean(c * c, axis=-1, keepdims=True)
    return c * lax.rsqrt(var + LN_EPS) * g + b


def _pack_bf16_pairs(x):
    w = x.shape[1] // 2
    bits = lax.bitcast_convert_type(x.astype(BF16).astype(F32), U32)
    return bits[:, :w] | lax.shift_right_logical(bits[:, w:], jnp.uint32(16))


def _unpack_hi(wd):
    return lax.bitcast_convert_type(wd & jnp.uint32(0xFFFF0000), F32)


def _unpack_lo(wd):
    return lax.bitcast_convert_type(lax.shift_left(wd, jnp.uint32(16)), F32)


def _params(sem):
    return pltpu.CompilerParams(dimension_semantics=sem, vmem_limit_bytes=VMEM_LIMIT)


def _ada_body(c_ref, w_ref, b_ref, o_ref):
    o_ref[0] = jnp.dot(c_ref[...], w_ref[0], preferred_element_type=F32,
                       precision=lax.Precision.HIGHEST) + b_ref[0]


def _ada_mod(c, w_ada, b_ada):
    L, D, N = w_ada.shape
    B = c.shape[0]
    tn = D
    return pl.pallas_call(
        _ada_body,
        grid=(L, N // tn),
        in_specs=[
            pl.BlockSpec((B, D), lambda l, j: (0, 0)),
            pl.BlockSpec((1, D, tn), lambda l, j: (l, 0, j)),
            pl.BlockSpec((1, 1, tn), lambda l, j: (l, 0, j)),
        ],
        out_specs=pl.BlockSpec((1, B, tn), lambda l, j: (l, 0, j)),
        out_shape=jax.ShapeDtypeStruct((L, B, N), F32),
        compiler_params=_params(("arbitrary", "arbitrary")),
        name="ada_mod",
    )(c, w_ada, b_ada.reshape(L, 1, N))


def _branch_a_body(x_ref, sc_ref, sh_ref, wa_ref, spw_ref, spb_ref, wba_ref, o_ref, a_scr):
    tm, D = x_ref.shape[1], x_ref.shape[2]
    gd = D // A_GROUPS
    h = (x_ref[0] * (1.0 + sc_ref[0]) + sh_ref[0]).astype(BF16)
    row = lax.broadcasted_iota(I32, (A_CHUNK, A_CHUNK), 0)
    col = lax.broadcasted_iota(I32, (A_CHUNK, A_CHUNK), 1)
    causal = row >= col
    for g in range(A_GROUPS):
        v = _gelu(jnp.dot(h, wa_ref[0, :, D + g * gd:D + (g + 1) * gd], preferred_element_type=F32))
        mu = jnp.mean(v, axis=-1, keepdims=True)
        vc = v - mu
        var = jnp.mean(vc * vc, axis=-1, keepdims=True)
        vn = (vc * lax.rsqrt(var + LN_EPS)).astype(BF16)
        u = _gelu(jnp.dot(h, wa_ref[0, :, g * gd:(g + 1) * gd], preferred_element_type=F32))
        ws = jnp.where(causal, spw_ref[0, g], 0.0).astype(BF16)
        bias = spb_ref[0, g]
        for c in range(tm // A_CHUNK):
            rows = slice(c * A_CHUNK, (c + 1) * A_CHUNK)
            sv = jnp.dot(ws, vn[rows], preferred_element_type=F32) + bias
            a_scr[rows, g * gd:(g + 1) * gd] = (u[rows] * sv).astype(BF16)
    ya = jnp.dot(a_scr[...], wba_ref[0], preferred_element_type=F32)
    gate = _sigmoid(jnp.dot(h, wa_ref[0, :, 2 * D:3 * D], preferred_element_type=F32))
    o_ref[0] = (gate * ya).astype(BF16)


def _branch_a(x, sc, sh, w_a, sp_w, sp_b, w_ba, layer):
    B, S, D = x.shape
    tm = TOKEN_TILE
    return pl.pallas_call(
        _branch_a_body,
        grid=(B, S // tm),
        in_specs=[
            pl.BlockSpec((1, tm, D), lambda b, i: (b, i, 0)),
            pl.BlockSpec((1, 1, D), lambda b, i: (b, 0, 0)),
            pl.BlockSpec((1, 1, D), lambda b, i: (b, 0, 0)),
            pl.BlockSpec((1, D, 3 * D), lambda b, i: (layer, 0, 0)),
            pl.BlockSpec((1, A_GROUPS, A_CHUNK, A_CHUNK), lambda b, i: (layer, 0, 0, 0)),
            pl.BlockSpec((1, A_GROUPS, A_CHUNK, 1), lambda b, i: (layer, 0, 0, 0)),
            pl.BlockSpec((1, D, D), lambda b, i: (layer, 0, 0)),
        ],
        out_specs=pl.BlockSpec((1, tm, D), lambda b, i: (b, i, 0)),
        out_shape=jax.ShapeDtypeStruct((B, S, D), BF16),
        scratch_shapes=[pltpu.VMEM((tm, D), BF16)],
        compiler_params=_params(("arbitrary", "arbitrary")),
        name="branch_a",
    )(x, sc, sh, w_a, sp_w, sp_b, w_ba)


def _rope(z, cos, s_up, s_dn):
    half = ROPE_DIM // 2
    parts = []
    for c in range(z.shape[1] // LANES):
        zc = z[:, c * LANES:(c + 1) * LANES]
        parts.append(zc * cos + pltpu.roll(zc, half, 1) * s_up + pltpu.roll(zc, LANES - half, 1) * s_dn)
    return jnp.concatenate(parts, axis=1)


def _qkv_body(x_ref, sc_ref, sh_ref, w_ref, cos_ref, sup_ref, sdn_ref, *rest):
    outs = rest[:9]
    gate_ref = rest[9]
    scr = rest[10]
    tm, D = x_ref.shape[1], x_ref.shape[2]
    gw = B_GROUP_WIDTH
    h = (x_ref[0] * (1.0 + sc_ref[0]) + sh_ref[0]).astype(BF16)
    cos, s_up, s_dn = cos_ref[...], sup_ref[...], sdn_ref[...]
    for g, d in enumerate(DILATIONS):
        for part in range(3):
            col = (3 * g + part) * gw
            z = jnp.dot(h, w_ref[0, :, col:col + gw], preferred_element_type=F32)
            if part < 2:
                z = _rope(z, cos, s_up, s_dn)
            if part == 0:
                z = z * (HEAD_DIM ** -0.5)
            o_ref = outs[3 * g + part]
            if d == 1:
                o_ref[0, 0] = z.astype(BF16)
            else:
                nc = gw // LANES
                for c in range(nc):
                    scr[c] = z[:, c * LANES:(c + 1) * LANES]
                for r in range(d):
                    o_ref[0, r] = jnp.concatenate(
                        [scr[c, pl.ds(r, tm // d, stride=d), :] for c in range(nc)], axis=1).astype(BF16)
    gate = _sigmoid(jnp.dot(h, w_ref[0, :, 9 * gw:9 * gw + D], preferred_element_type=F32))
    gate_ref[0] = gate.astype(BF16)


def _qkv(x, sc, sh, w_c, cos_t, sup_t, sdn_t, layer):
    B, S, D = x.shape
    tm = TOKEN_TILE
    gw = B_GROUP_WIDTH
    nw = w_c.shape[2]
    out_shapes, out_specs = [], []
    for d in DILATIONS:
        for _ in range(3):
            out_shapes.append(jax.ShapeDtypeStruct((B, d, S // d, gw), BF16))
            out_specs.append(pl.BlockSpec((1, d, tm // d, gw), lambda b, i: (b, 0, i, 0)))
    out_shapes.append(jax.ShapeDtypeStruct((B, S, D), BF16))
    out_specs.append(pl.BlockSpec((1, tm, D), lambda b, i: (b, i, 0)))
    return pl.pallas_call(
        _qkv_body,
        grid=(B, S // tm),
        in_specs=[
            pl.BlockSpec((1, tm, D), lambda b, i: (b, i, 0)),
            pl.BlockSpec((1, 1, D), lambda b, i: (b, 0, 0)),
            pl.BlockSpec((1, 1, D), lambda b, i: (b, 0, 0)),
            pl.BlockSpec((1, D, nw), lambda b, i: (layer, 0, 0), pipeline_mode=pl.Buffered(1)),
            pl.BlockSpec((tm, LANES), lambda b, i: (i, 0)),
            pl.BlockSpec((tm, LANES), lambda b, i: (i, 0)),
            pl.BlockSpec((tm, LANES), lambda b, i: (i, 0)),
        ],
        out_specs=out_specs,
        out_shape=out_shapes,
        scratch_shapes=[pltpu.VMEM((gw // LANES, tm, LANES), F32)],
        compiler_params=_params(("arbitrary", "arbitrary")),
        name="qkv",
    )(x, sc, sh, w_c, cos_t, sup_t, sdn_t)


def _attention_body(*refs):
    ins = refs[:15]
    o_ref = refs[15]
    o_scr, m_scr, l_scr, bias_scr, p_scr = refs[16:21]
    n = pl.program_id(1)
    blk = ATT_SPAN
    qi = lax.broadcasted_iota(I32, (blk, 2 * blk), 0)
    kj = lax.broadcasted_iota(I32, (blk, 2 * blk), 1)
    band = (kj >= qi) & (kj <= qi + ATT_SPAN)
    neg = jnp.float32(-jnp.inf)
    bias_scr[0] = jnp.where(band, 0.0, neg)
    bias_scr[1] = jnp.where(band & ((kj >= blk) | (n > 0)), 0.0, neg)
    lane = lax.broadcasted_iota(I32, (blk, LANES), 1)
    head0 = lane < HEAD_DIM
    ones = jnp.ones((2 * blk, LANES), BF16)

    def store(scr, g, d, r, j, val):
        view = scr.at[g, pl.ds(j * blk * d, blk * d)]
        if d == 1:
            view[...] = val
        else:
            view[pl.ds(r, blk, stride=d), :] = val

    def window(prev_ref, cur_ref, r, j):
        if j == 0:
            return jnp.concatenate([prev_ref[0, r], cur_ref[0, r, 0:blk]], axis=0)
        return cur_ref[0, r, (j - 1) * blk:(j + 1) * blk]

    chunk_id = 0
    for g, d in enumerate(DILATIONS):
        q_ref, kc_ref, kp_ref, vc_ref, vp_ref = ins[5 * g:5 * g + 5]
        nj = ATT_BLOCK // d // blk
        units = [(r, j) for r in range(d) for j in range(nj)]
        for c0 in range(0, len(units), ATT_CHUNK):
            chunk = units[c0:c0 + ATT_CHUNK]
            pbase = (chunk_id % 2) * 2 * ATT_CHUNK
            chunk_id += 1
            for ui, (r, j) in enumerate(chunk):
                q = q_ref[0, r, j * blk:(j + 1) * blk]
                kk = window(kp_ref, kc_ref, r, j)
                bias = bias_scr[1 if j == 0 else 0]
                zero = jnp.zeros_like(q)
                ms = []
                for h, qh in enumerate((jnp.where(head0, q, zero), jnp.where(head0, zero, q))):
                    s = lax.dot_general(qh, kk, (((1,), (1,)), ((), ())), preferred_element_type=F32) + bias
                    m = jnp.max(s, axis=1, keepdims=True)
                    p_scr[pbase + 2 * ui + h] = jnp.exp(s - m).astype(BF16)
                    ms.append(m)
                store(m_scr, g, d, r, j, jnp.where(head0, ms[0], ms[1]))
            for ui, (r, j) in enumerate(chunk):
                ve = jnp.concatenate([window(vp_ref, vc_ref, r, j), ones], axis=1)
                oe = [jnp.dot(p_scr[pbase + 2 * ui + h], ve, preferred_element_type=F32) for h in range(2)]
                store(o_scr, g, d, r, j, jnp.where(head0, oe[0][:, :LANES], oe[1][:, :LANES]))
                store(l_scr, g, d, r, j, jnp.where(head0, oe[0][:, LANES:], oe[1][:, LANES:]))

    def merge(c, carry):
        rows = pl.ds(pl.multiple_of(c * blk, blk), blk)
        ms = [m_scr[g, rows] for g in range(3)]
        m_ref = jnp.maximum(jnp.maximum(ms[0], ms[1]), ms[2])
        num = jnp.zeros((blk, LANES), F32)
        den = jnp.zeros((blk, LANES), F32)
        for g in range(3):
            w = jnp.exp(ms[g] - m_ref)
            num = num + w * o_scr[g, rows]
            den = den + w * l_scr[g, rows]
        o_ref[0, rows] = (num / den).astype(BF16)
        return carry
    lax.fori_loop(0, ATT_BLOCK // blk, merge, 0)


def _attention(qkv):
    B = qkv[0].shape[0]
    S = qkv[0].shape[2]
    gw = B_GROUP_WIDTH
    blk = ATT_SPAN
    nblk = S // ATT_BLOCK
    in_specs, args = [], []
    for g, d in enumerate(DILATIONS):
        rows = ATT_BLOCK // d
        per = rows // blk
        cur = pl.BlockSpec((1, d, rows, LANES), lambda b, n, hp: (b, 0, n, hp))
        prev = pl.BlockSpec((1, d, blk, LANES),
                            lambda b, n, hp, per=per: (b, 0, jnp.maximum(n * per - 1, 0), hp))
        q, k, v = qkv[3 * g:3 * g + 3]
        in_specs += [cur, cur, prev, cur, prev]
        args += [q, k, k, v, v]
    return pl.pallas_call(
        _attention_body,
        grid=(B, nblk, gw // LANES),
        in_specs=in_specs,
        out_specs=pl.BlockSpec((1, ATT_BLOCK, LANES), lambda b, n, hp: (b, n, hp)),
        out_shape=jax.ShapeDtypeStruct((B, S, gw), BF16),
        scratch_shapes=[pltpu.VMEM((3, ATT_BLOCK, LANES), F32)] * 3 + [
            pltpu.VMEM((2, blk, 2 * blk), F32),
            pltpu.VMEM((4 * ATT_CHUNK, blk, 2 * blk), BF16)],
        compiler_params=_params(("arbitrary", "arbitrary", "arbitrary")),
        name="attention",
    )(*args)


def _post_mixer_body(alpha, x_ref, ya_ref, gb_ref, ob_ref, wbb_ref, wout_ref, gm_ref, lng_ref, lnb_ref,
                     sc_ref, sh_ref, wr_ref, br_ref,
                     x1_ref, h2_ref, pos_ref, wts_ref, cnt_ref):
    tt = x_ref.shape[1]
    yb = jnp.dot(ob_ref[0], wbb_ref[0], preferred_element_type=F32)
    merged = (ya_ref[0].astype(F32) + gb_ref[0].astype(F32) * yb).astype(BF16)
    y = jnp.dot(merged, wout_ref[0], preferred_element_type=F32)
    x1 = _layer_norm(alpha * x_ref[0] + (1.0 + gm_ref[0]) * y, lng_ref[0], lnb_ref[0])
    x1_ref[0] = x1
    h2 = x1 * (1.0 + sc_ref[0]) + sh_ref[0]

    logits = lax.dot_general(wr_ref[0], h2, (((1,), (1,)), ((), ())), preferred_element_type=F32,
                             precision=lax.Precision.HIGHEST) + br_ref[0]
    eid = lax.broadcasted_iota(I32, (N_EXPERTS, tt), 0)
    vals, hots = [], []
    cur = logits
    for k in range(TOP_K):
        m = jnp.max(cur, axis=0, keepdims=True)
        sel = jnp.min(jnp.where(cur == m, eid, N_EXPERTS), axis=0, keepdims=True)
        hot = eid == sel
        vals.append(m)
        hots.append(hot)
        cur = jnp.where(hot, -jnp.inf, cur)
    ex = [jnp.exp(v - vals[0]) for v in vals]
    tot = ex[0] + ex[1] + ex[2] + ex[3]
    for k in range(TOP_K):
        wts_ref[k:k + 1, :] = ex[k] / tot

    member = jnp.zeros((N_EXPERTS, tt), F32)
    for hot in hots:
        member = member + hot.astype(F32)
    before = (lax.broadcasted_iota(I32, (tt, tt), 0) < lax.broadcasted_iota(I32, (tt, tt), 1))
    excl = jnp.dot(member.astype(BF16), before.astype(BF16), preferred_element_type=F32)
    cnt = jnp.broadcast_to(jnp.sum(member, axis=1, keepdims=True), (N_EXPERTS, LANES))
    cnt_ref[...] = cnt
    lower = (lax.broadcasted_iota(I32, (N_EXPERTS, N_EXPERTS), 1)
             < lax.broadcasted_iota(I32, (N_EXPERTS, N_EXPERTS), 0)).astype(F32)
    cnt_al = jnp.ceil(cnt * (1.0 / RUN_ALIGN)) * RUN_ALIGN
    off = jnp.dot(lower, cnt_al, preferred_element_type=F32, precision=lax.Precision.HIGHEST)[:, 0:1]
    base = off + excl
    for k in range(TOP_K):
        pos_ref[k:k + 1, :] = jnp.sum(jnp.where(hots[k], base, 0.0), axis=0, keepdims=True).astype(I32)
    h2_ref[...] = h2.astype(BF16)


def _post_mixer(x, ya, gb, ob, w_bb, w_out, gm, ln_g, ln_b, sc, sh, w_rt, b_r, layer, depth):
    B, S, D = x.shape
    tt = TOKEN_TILE
    nt = S // tt
    T = B * S
    gw = B_GROUP_WIDTH
    tok = lambda b, i: (b, i, 0)
    per_b = lambda b, i: (b, 0, 0)
    per_l3 = lambda b, i: (layer, 0, 0)
    flat = lambda b, i: (0, b * nt + i)
    return pl.pallas_call(
        functools.partial(_post_mixer_body, (2.0 * depth) ** 0.25),
        grid=(B, nt),
        in_specs=[
            pl.BlockSpec((1, tt, D), tok),
            pl.BlockSpec((1, tt, D), tok),
            pl.BlockSpec((1, tt, D), tok),
            pl.BlockSpec((1, tt, gw), tok),
            pl.BlockSpec((1, gw, D), per_l3),
            pl.BlockSpec((1, D, D), per_l3),
            pl.BlockSpec((1, 1, D), per_b),
            pl.BlockSpec((1, 1, D), per_l3),
            pl.BlockSpec((1, 1, D), per_l3),
            pl.BlockSpec((1, 1, D), per_b),
            pl.BlockSpec((1, 1, D), per_b),
            pl.BlockSpec((1, N_EXPERTS, D), per_l3),
            pl.BlockSpec((1, N_EXPERTS, 1), per_l3),
        ],
        out_specs=[
            pl.BlockSpec((1, tt, D), tok),
            pl.BlockSpec((tt, D), lambda b, i: (b * nt + i, 0)),
            pl.BlockSpec((TOP_K, tt), flat),
            pl.BlockSpec((TOP_K, tt), flat),
            pl.BlockSpec((N_EXPERTS, LANES), lambda b, i: (b * nt + i, 0)),
        ],
        out_shape=[
            jax.ShapeDtypeStruct((B, S, D), F32),
            jax.ShapeDtypeStruct((T, D), BF16),
            jax.ShapeDtypeStruct((TOP_K, T), I32),
            jax.ShapeDtypeStruct((TOP_K, T), F32),
            jax.ShapeDtypeStruct((B * nt * N_EXPERTS, LANES), F32),
        ],
        compiler_params=_params(("arbitrary", "arbitrary")),
        name="post_mixer",
    )(x, ya, gb, ob, w_bb, w_out, gm, ln_g, ln_b, sc, sh, w_rt, b_r)


def _run_pieces(n, fn):
    for bit in reversed(range(RUN_ALIGN.bit_length() - 1, RUN_BITS)):
        size = 1 << bit
        done = pl.multiple_of((n >> (bit + 1)) << (bit + 1), RUN_ALIGN)

        @pl.when((n & size) != 0)
        def _():
            fn(done, size)


def _each_run(step, len_ref, a_off_ref, b_off_ref, make_copy, action):
    def run(e, carry):
        i = step * N_EXPERTS + e
        a0, b0 = a_off_ref[i], b_off_ref[i]
        _run_pieces(len_ref[i], lambda off, size: action(make_copy(
            pl.multiple_of(a0 + off, RUN_ALIGN), pl.multiple_of(b0 + off, RUN_ALIGN), size)))
        return carry
    lax.fori_loop(0, N_EXPERTS, run, 0)


def _dispatch_body(tile_off_ref, slot_off_ref, len_ref, ends_ref, padded_ref, h_ref, pos_ref, xs_ref,
                   zbuf, rbuf, fill_sem, sem):
    t = pl.program_id(0)
    last = pl.num_programs(0) - 1
    tm = zbuf.shape[0]
    tt = h_ref.shape[0]

    @pl.when(t == 0)
    def _():
        zbuf[...] = jnp.zeros_like(zbuf)
        for e in range(N_EXPERTS):
            @pl.when(padded_ref[e] > 0)
            def _():
                start = pl.multiple_of(ends_ref[e] - tm, tm)
                cp = pltpu.make_async_copy(zbuf, xs_ref.at[pl.ds(start, tm)], fill_sem)
                cp.start()
                cp.wait()

        def fill_unused(i, carry):
            cp = pltpu.make_async_copy(zbuf, xs_ref.at[pl.ds(pl.multiple_of(i * tm, tm), tm)], fill_sem)
            cp.start()
            cp.wait()
            return carry
        lax.fori_loop(ends_ref[N_EXPERTS - 1] // tm, xs_ref.shape[0] // tm, fill_unused, 0)

    buf = t % 2
    h = h_ref[...]
    pos = [pos_ref[k:k + 1, :] for k in range(TOP_K)]
    for c in range(rbuf.shape[1] // PERM_ROWS):
        p_id = lax.broadcasted_iota(I32, (PERM_ROWS, tt), 0) + c * PERM_ROWS
        sel = (p_id == pos[0]) | (p_id == pos[1]) | (p_id == pos[2]) | (p_id == pos[3])
        rows = jnp.dot(jnp.where(sel, 1.0, 0.0).astype(BF16), h, preferred_element_type=F32)
        rbuf[buf, c * PERM_ROWS:(c + 1) * PERM_ROWS, :] = _pack_bf16_pairs(rows)

    def runs(s, action):
        def make_copy(tile_row, slot_row, size):
            return pltpu.make_async_copy(rbuf.at[s % 2, pl.ds(tile_row, size)], xs_ref.at[pl.ds(slot_row, size)],
                                         sem.at[s % 2])
        _each_run(s, len_ref, tile_off_ref, slot_off_ref, make_copy, action)

    runs(t, lambda cp: cp.start())

    @pl.when(t > 0)
    def _():
        runs(t - 1, lambda cp: cp.wait())

    @pl.when(t == last)
    def _():
        runs(t, lambda cp: cp.wait())


def _dispatch(h2, pos, tile_off, slot_off, run_len, ends, padded, n_slots):
    T, D = h2.shape
    tt = TOKEN_TILE
    grid_spec = pltpu.PrefetchScalarGridSpec(
        num_scalar_prefetch=5,
        grid=(T // tt,),
        in_specs=[pl.BlockSpec((tt, D), lambda t, *_: (t, 0)),
                  pl.BlockSpec((TOP_K, tt), lambda t, *_: (0, t))],
        out_specs=pl.BlockSpec(memory_space=pl.ANY),
        scratch_shapes=[pltpu.VMEM((EXPERT_TILE, D // 2), U32), pltpu.VMEM((2, RUN_TILE_ROWS, D // 2), U32),
                        pltpu.SemaphoreType.DMA, pltpu.SemaphoreType.DMA((2,))],
    )
    return pl.pallas_call(
        _dispatch_body,
        grid_spec=grid_spec,
        out_shape=jax.ShapeDtypeStruct((n_slots, D // 2), U32),
        compiler_params=pltpu.CompilerParams(dimension_semantics=("arbitrary",), vmem_limit_bytes=VMEM_LIMIT,
                                             has_side_effects=True),
        name="dispatch",
    )(tile_off, slot_off, run_len, ends, padded, h2, pos)


def _experts_body(te_ref, nact_ref, xs_ref, wup_ref, bup_ref, wdn_ref, bdn_ref, ys_ref, wup_s, wdn_s):
    i = pl.program_id(0)
    ff = wdn_ref.shape[1]

    @pl.when((i == 0) | (te_ref[i] != te_ref[jnp.maximum(i - 1, 0)]))
    def _():
        n = 2 * LANES
        src = lax.broadcasted_iota(I32, (n, n), 0)
        dst = lax.broadcasted_iota(I32, (n, n), 1)
        perm = (src == jnp.where(dst < LANES, 2 * dst, 2 * (dst - LANES) + 1)).astype(BF16)
        for blk in range(wup_ref.shape[2] // n):
            cols = slice(blk * n, (blk + 1) * n)
            wup_s[:, cols] = jnp.dot(wup_ref[0, :, cols].astype(BF16), perm,
                                     preferred_element_type=F32).astype(BF16)
        wdn_s[...] = wdn_ref[0].astype(BF16)

    @pl.when(i < nact_ref[0])
    def _():
        wd = xs_ref[...]
        x = jnp.concatenate([_unpack_hi(wd), _unpack_lo(wd)], axis=1).astype(BF16)
        hh = jnp.dot(x, wup_s[...], preferred_element_type=F32) + bup_ref[0]
        acts = []
        for blk in range(ff // LANES):
            glu = jnp.minimum(hh[:, 2 * blk * LANES:(2 * blk + 1) * LANES], SWIGLU_LIMIT)
            lin = jnp.clip(hh[:, (2 * blk + 1) * LANES:(2 * blk + 2) * LANES], -SWIGLU_LIMIT, SWIGLU_LIMIT)
            acts.append((glu * _sigmoid(SWIGLU_ALPHA * glu) * (lin + 1.0)).astype(BF16))
        act = jnp.concatenate(acts, axis=1)
        y = jnp.dot(act, wdn_s[...], preferred_element_type=F32) + bdn_ref[0]
        ys_ref[...] = _pack_bf16_pairs(y)

    @pl.when(i >= nact_ref[0])
    def _():
        ys_ref[...] = jnp.zeros_like(ys_ref)


def _experts(xs, tile_expert, n_active, w_up, b_up, w_dn, b_dn, layer):
    P, W = xs.shape
    tm = EXPERT_TILE
    D = 2 * W
    ff2 = w_up.shape[2]
    ff = w_dn.shape[1]
    ex = lambda i, te, na: (layer * N_EXPERTS + te[i], 0, 0)
    grid_spec = pltpu.PrefetchScalarGridSpec(
        num_scalar_prefetch=2,
        grid=(P // tm,),
        in_specs=[
            pl.BlockSpec((tm, W), lambda i, te, na: (jnp.where(i < na[0], i, 0), 0)),
            pl.BlockSpec((1, D, ff2), ex),
            pl.BlockSpec((1, 1, ff2), ex),
            pl.BlockSpec((1, ff, D), ex),
            pl.BlockSpec((1, 1, D), ex),
        ],
        out_specs=pl.BlockSpec((tm, W), lambda i, te, na: (i, 0)),
        scratch_shapes=[pltpu.VMEM((D, ff2), BF16), pltpu.VMEM((ff, D), BF16)],
    )
    return pl.pallas_call(
        _experts_body,
        grid_spec=grid_spec,
        out_shape=jax.ShapeDtypeStruct((P, W), U32),
        compiler_params=_params(("arbitrary",)),
        name="experts",
    )(tile_expert, n_active, xs, w_up, b_up, w_dn, b_dn)


def _combine_body(alpha, tile_off_ref, slot_off_ref, len_ref, x_ref, ys_ref, pos_ref, wts_ref, gm_ref, lng_ref,
                  lnb_ref, o_ref, ybuf, sem):
    tt = x_ref.shape[1]
    n = ybuf.shape[1]
    step = pl.program_id(0) * pl.num_programs(1) + pl.program_id(1)
    n_steps = pl.num_programs(0) * pl.num_programs(1)

    def runs(s, action):
        buf = s % 2

        def make_copy(tile_row, slot_row, size):
            return pltpu.make_async_copy(ys_ref.at[pl.ds(slot_row, size)], ybuf.at[buf, pl.ds(tile_row, size)],
                                         sem.at[buf])
        _each_run(s, len_ref, tile_off_ref, slot_off_ref, make_copy, action)

    def fetch(s):
        ybuf[s % 2, TOP_K * tt:n, :] = jnp.zeros((n - TOP_K * tt, ybuf.shape[2]), U32)
        runs(s, lambda cp: cp.start())

    @pl.when(step == 0)
    def _():
        fetch(step)

    @pl.when(step + 1 < n_steps)
    def _():
        fetch(step + 1)

    runs(step, lambda cp: cp.wait())

    p_id = lax.broadcasted_iota(I32, (tt, n), 1)
    a = jnp.zeros((tt, n), F32)
    for k in reversed(range(TOP_K)):
        a = jnp.where(p_id == pos_ref[:, k:k + 1], wts_ref[:, k:k + 1], a)
    a = a.astype(BF16)
    wd = ybuf[step % 2]
    hi = jnp.dot(a, _unpack_hi(wd).astype(BF16), preferred_element_type=F32)
    lo = jnp.dot(a, _unpack_lo(wd).astype(BF16), preferred_element_type=F32)
    y = jnp.concatenate([hi, lo], axis=1)
    o_ref[0] = _layer_norm(alpha * x_ref[0] + (1.0 + gm_ref[0]) * y, lng_ref[0], lnb_ref[0])


def _combine(x1, ys, tile_off, slot_off, run_len, pos_t, wts_t, gm, ln_g, ln_b, layer, depth):
    B, S, D = x1.shape
    tt = TOKEN_TILE
    nt = S // tt
    grid_spec = pltpu.PrefetchScalarGridSpec(
        num_scalar_prefetch=3,
        grid=(B, nt),
        in_specs=[
            pl.BlockSpec((1, tt, D), lambda b, i, *_: (b, i, 0)),
            pl.BlockSpec(memory_space=pl.ANY),
            pl.BlockSpec((tt, TOP_K), lambda b, i, *_: (b * nt + i, 0)),
            pl.BlockSpec((tt, TOP_K), lambda b, i, *_: (b * nt + i, 0)),
            pl.BlockSpec((1, 1, D), lambda b, i, *_: (b, 0, 0)),
            pl.BlockSpec((1, 1, D), lambda b, i, *_: (layer, 0, 0)),
            pl.BlockSpec((1, 1, D), lambda b, i, *_: (layer, 0, 0)),
        ],
        out_specs=pl.BlockSpec((1, tt, D), lambda b, i, *_: (b, i, 0)),
        scratch_shapes=[pltpu.VMEM((2, RUN_TILE_ROWS, D // 2), U32), pltpu.SemaphoreType.DMA((2,))],
    )
    return pl.pallas_call(
        functools.partial(_combine_body, (2.0 * depth) ** 0.25),
        grid_spec=grid_spec,
        out_shape=jax.ShapeDtypeStruct((B, S, D), F32),
        compiler_params=_params(("arbitrary", "arbitrary")),
        name="combine",
    )(tile_off, slot_off, run_len, x1, ys, pos_t, wts_t, gm, ln_g, ln_b)


def _rope_tables(S):
    half = ROPE_DIM // 2
    inv = ROPE_THETA ** (-jnp.arange(half, dtype=F32) / half)
    ang = jnp.arange(S, dtype=F32)[:, None] * inv[None, :]
    cos, sin = jnp.cos(ang), jnp.sin(ang)
    lane = jnp.arange(LANES) % HEAD_DIM
    pick = lane % half
    cos_l, sin_l = cos[:, pick], sin[:, pick]
    cos_t = jnp.where(lane < ROPE_DIM, cos_l, 1.0)
    s_up = jnp.where((lane >= half) & (lane < ROPE_DIM), sin_l, 0.0)
    s_dn = jnp.where(lane < half, -sin_l, 0.0)
    return cos_t.astype(F32), s_up.astype(F32), s_dn.astype(F32)


def kernel(x, c, w_ada, b_ada, w_in, spatial_w, spatial_b, w_branch_a, w_branch_b, w_out, ln1_g, ln1_b,
           w_router, b_router, w_up, b_up, w_down, b_down, ln2_g, ln2_b):
    B, S, D = x.shape
    L = w_ada.shape[0]
    T = B * S
    gw = B_GROUP_WIDTH
    nq = len(DILATIONS) * gw
    assert S % ATT_BLOCK == 0 and S % TOKEN_TILE == 0 and D % (2 * LANES) == 0

    q0, k0, v0, g0 = 2 * D, 2 * D + nq, 2 * D + 2 * nq, 2 * D + 3 * nq
    w_a = jnp.concatenate([w_in[:, :, :2 * D], w_in[:, :, g0:g0 + D]], axis=2).astype(BF16)
    cols = []
    for g in range(len(DILATIONS)):
        for base in (q0, k0, v0):
            cols.append(w_in[:, :, base + g * gw:base + (g + 1) * gw])
    cols.append(w_in[:, :, g0 + D:g0 + 2 * D])
    w_c = jnp.concatenate(cols, axis=2).astype(BF16)
    w_ba = w_branch_a.astype(BF16)
    w_bb = w_branch_b.astype(BF16)
    w_o = w_out.astype(BF16)
    sp_b = spatial_b[..., None]
    w_rt = jnp.swapaxes(w_router, 1, 2)
    b_r = b_router[..., None]
    ff = w_down.shape[2]
    w_up_p = w_up.reshape(L * N_EXPERTS, D, 2 * ff)
    b_up_p = jnp.swapaxes(b_up.reshape(L * N_EXPERTS, 1, ff // LANES, LANES, 2), 3, 4).reshape(
        L * N_EXPERTS, 1, 2 * ff)
    w_dn_p = w_down.reshape(L * N_EXPERTS, ff, D)
    b_dn_p = b_down.reshape(L * N_EXPERTS, 1, D)
    ln1g, ln1b = ln1_g[:, None, :], ln1_b[:, None, :]
    ln2g, ln2b = ln2_g[:, None, :], ln2_b[:, None, :]
    cos_t, sup_t, sdn_t = _rope_tables(S)

    mod = _ada_mod(c, w_ada, b_ada)
    n_slots = T * TOP_K + (T // TOKEN_TILE) * N_EXPERTS * RUN_ALIGN + N_EXPERTS * EXPERT_TILE
    n_tiles = n_slots // EXPERT_TILE

    for l in range(L):
        sh1, sc1, g1, sh2, sc2, g2 = [mod[l, :, None, i * D:(i + 1) * D] for i in range(6)]
        ya = _branch_a(x, sc1, sh1, w_a, spatial_w, sp_b, w_ba, l)
        *qkv, gb = _qkv(x, sc1, sh1, w_c, cos_t, sup_t, sdn_t, l)
        ob = _attention(qkv)
        x1, h2, pos, wts, cnt = _post_mixer(x, ya, gb, ob, w_bb, w_o, g1, ln1g, ln1b, sc2, sh2, w_rt, b_r, l, L)
        n_tok_tiles = T // TOKEN_TILE
        tc = cnt.reshape(n_tok_tiles, N_EXPERTS, LANES)[:, :, 0].astype(I32)
        tc = ((tc + RUN_ALIGN - 1) // RUN_ALIGN) * RUN_ALIGN
        off_in_tile = jnp.cumsum(tc, axis=1) - tc
        seen_before = jnp.cumsum(tc, axis=0) - tc
        counts = jnp.sum(tc, axis=0)
        padded = ((counts + EXPERT_TILE - 1) // EXPERT_TILE) * EXPERT_TILE
        ends = jnp.cumsum(padded).astype(I32)
        starts = ends - padded
        tile_off = off_in_tile.astype(I32).reshape(-1)
        slot_off = (starts[None, :] + seen_before).astype(I32).reshape(-1)
        run_len = tc.reshape(-1)
        n_active = (ends[-1] // EXPERT_TILE).astype(I32)
        tile_start = jnp.arange(n_tiles, dtype=I32) * EXPERT_TILE
        te = jnp.sum((ends[None, :] <= tile_start[:, None]).astype(I32), axis=1)
        te_last = jnp.sum((ends <= (n_active - 1) * EXPERT_TILE).astype(I32))
        te = jnp.minimum(jnp.where(tile_start < ends[-1], te, te_last), N_EXPERTS - 1).astype(I32)

        padded = padded.astype(I32)
        xs = _dispatch(h2, pos, tile_off, slot_off, run_len, ends, padded, n_slots)
        ys = _experts(xs, te, n_active.reshape(1), w_up_p, b_up_p, w_dn_p, b_dn_p, l)
        x = _combine(x1, ys, tile_off, slot_off, run_len, pos.T, wts.T, g2, ln2g, ln2b, l, L)
    return x
```

```python
import functools
import math

import jax
import jax.numpy as jnp
from jax import lax
from jax.experimental import pallas as pl
from jax.experimental.pallas import tpu as pltpu

BF16 = jnp.bfloat16
F32 = jnp.float32
U32 = jnp.uint32
I32 = jnp.int32

A_CHUNK = 128
A_GROUPS = 4
HEAD_DIM = 64
B_HEADS = 8
DILATIONS = (1, 4, 16)
ATT_SPAN = 128
B_GROUP_WIDTH = B_HEADS * HEAD_DIM
ROPE_DIM = HEAD_DIM // 4
ROPE_THETA = 500000.0
N_EXPERTS = 32
TOP_K = 4
SWIGLU_LIMIT = 7.0
SWIGLU_ALPHA = 1.702
LN_EPS = 1e-5

LANES = 128
VMEM_LIMIT = 56 * 1024 * 1024
TOKEN_TILE = 512
MIXER_TILE = 1024
ATT_BLOCK = 128 * DILATIONS[-1]
EXPERT_TILE = 512
ATT_CHUNK = 4
RUN_BITS = TOKEN_TILE.bit_length()
RUN_ALIGN = 8
RUN_TILE_ROWS = TOP_K * TOKEN_TILE + N_EXPERTS * RUN_ALIGN
PERM_ROWS = 256


def _gelu(x):
    return 0.5 * x * (1.0 + jnp.tanh(math.sqrt(2.0 / math.pi) * (x + 0.044715 * (x * x * x))))


def _sigmoid(x):
    return 1.0 / (1.0 + jnp.exp(-x))


def _layer_norm(r, g, b):
    mu = jnp.mean(r, axis=-1, keepdims=True)
    c = r - mu
    var = jnp.mean(c * c, axis=-1, keepdims=True)
    return c * lax.rsqrt(var + LN_EPS) * g + b


def _pack_bf16_pairs(x):
    w = x.shape[1] // 2
    bits = lax.bitcast_convert_type(x.astype(BF16).astype(F32), U32)
    return bits[:, :w] | lax.shift_right_logical(bits[:, w:], jnp.uint32(16))


def _unpack_hi(wd):
    return lax.bitcast_convert_type(wd & jnp.uint32(0xFFFF0000), F32)


def _unpack_lo(wd):
    return lax.bitcast_convert_type(lax.shift_left(wd, jnp.uint32(16)), F32)


def _params(sem):
    return pltpu.CompilerParams(dimension_semantics=sem, vmem_limit_bytes=VMEM_LIMIT)


def _ada_body(c_ref, w_ref, b_ref, o_ref):
    o_ref[0] = jnp.dot(c_ref[...], w_ref[0], preferred_element_type=F32,
                       precision=lax.Precision.HIGHEST) + b_ref[0]


def _ada_mod(c, w_ada, b_ada):
    L, D, N = w_ada.shape
    B = c.shape[0]
    tn = D
    return pl.pallas_call(
        _ada_body,
        grid=(L, N // tn),
        in_specs=[
            pl.BlockSpec((B, D), lambda l, j: (0, 0)),
            pl.BlockSpec((1, D, tn), lambda l, j: (l, 0, j)),
            pl.BlockSpec((1, 1, tn), lambda l, j: (l, 0, j)),
        ],
        out_specs=pl.BlockSpec((1, B, tn), lambda l, j: (l, 0, j)),
        out_shape=jax.ShapeDtypeStruct((L, B, N), F32),
        compiler_params=_params(("arbitrary", "arbitrary")),
        name="ada_mod",
    )(c, w_ada, b_ada.reshape(L, 1, N))


def _branch_a_body(x_ref, sc_ref, sh_ref, wa_ref, spw_ref, spb_ref, wba_ref, o_ref, a_scr):
    tm, D = x_ref.shape[1], x_ref.shape[2]
    gd = D // A_GROUPS
    h = (x_ref[0] * (1.0 + sc_ref[0]) + sh_ref[0]).astype(BF16)
    row = lax.broadcasted_iota(I32, (A_CHUNK, A_CHUNK), 0)
    col = lax.broadcasted_iota(I32, (A_CHUNK, A_CHUNK), 1)
    causal = row >= col
    for g in range(A_GROUPS):
        v = _gelu(jnp.dot(h, wa_ref[0, :, D + g * gd:D + (g + 1) * gd], preferred_element_type=F32))
        mu = jnp.mean(v, axis=-1, keepdims=True)
        vc = v - mu
        var = jnp.mean(vc * vc, axis=-1, keepdims=True)
        vn = (vc * lax.rsqrt(var + LN_EPS)).astype(BF16)
        u = _gelu(jnp.dot(h, wa_ref[0, :, g * gd:(g + 1) * gd], preferred_element_type=F32))
        ws = jnp.where(causal, spw_ref[0, g], 0.0).astype(BF16)
        bias = spb_ref[0, g]
        for c in range(tm // A_CHUNK):
            rows = slice(c * A_CHUNK, (c + 1) * A_CHUNK)
            sv = jnp.dot(ws, vn[rows], preferred_element_type=F32) + bias
            a_scr[rows, g * gd:(g + 1) * gd] = (u[rows] * sv).astype(BF16)
    ya = jnp.dot(a_scr[...], wba_ref[0], preferred_element_type=F32)
    gate = _sigmoid(jnp.dot(h, wa_ref[0, :, 2 * D:3 * D], preferred_element_type=F32))
    o_ref[0] = (gate * ya).astype(BF16)


def _branch_a(x, sc, sh, w_a, sp_w, sp_b, w_ba, layer):
    B, S, D = x.shape
    tm = MIXER_TILE
    return pl.pallas_call(
        _branch_a_body,
        grid=(B, S // tm),
        in_specs=[
            pl.BlockSpec((1, tm, D), lambda b, i: (b, i, 0)),
            pl.BlockSpec((1, 1, D), lambda b, i: (b, 0, 0)),
            pl.BlockSpec((1, 1, D), lambda b, i: (b, 0, 0)),
            pl.BlockSpec((1, D, 3 * D), lambda b, i: (layer, 0, 0)),
            pl.BlockSpec((1, A_GROUPS, A_CHUNK, A_CHUNK), lambda b, i: (layer, 0, 0, 0)),
            pl.BlockSpec((1, A_GROUPS, A_CHUNK, 1), lambda b, i: (layer, 0, 0, 0)),
            pl.BlockSpec((1, D, D), lambda b, i: (layer, 0, 0)),
        ],
        out_specs=pl.BlockSpec((1, tm, D), lambda b, i: (b, i, 0)),
        out_shape=jax.ShapeDtypeStruct((B, S, D), BF16),
        scratch_shapes=[pltpu.VMEM((tm, D), BF16)],
        compiler_params=_params(("arbitrary", "arbitrary")),
        name="branch_a",
    )(x, sc, sh, w_a, sp_w, sp_b, w_ba)


def _rope(z, cos, s_up, s_dn):
    half = ROPE_DIM // 2
    parts = []
    for c in range(z.shape[1] // LANES):
        zc = z[:, c * LANES:(c + 1) * LANES]
        parts.append(zc * cos + pltpu.roll(zc, half, 1) * s_up + pltpu.roll(zc, LANES - half, 1) * s_dn)
    return jnp.concatenate(parts, axis=1)


def _qkv_body(x_ref, sc_ref, sh_ref, w_ref, cos_ref, sup_ref, sdn_ref, *rest):
    outs = rest[:9]
    gate_ref = rest[9]
    scr = rest[10]
    tm, D = x_ref.shape[1], x_ref.shape[2]
    gw = B_GROUP_WIDTH
    h = (x_ref[0] * (1.0 + sc_ref[0]) + sh_ref[0]).astype(BF16)
    cos, s_up, s_dn = cos_ref[...], sup_ref[...], sdn_ref[...]
    for g, d in enumerate(DILATIONS):
        for part in range(3):
            col = (3 * g + part) * gw
            z = jnp.dot(h, w_ref[0, :, col:col + gw], preferred_element_type=F32)
            if part < 2:
                z = _rope(z, cos, s_up, s_dn)
            if part == 0:
                z = z * (HEAD_DIM ** -0.5)
            o_ref = outs[3 * g + part]
            if d == 1:
                o_ref[0, 0] = z.astype(BF16)
            else:
                nc = gw // LANES
                for c in range(nc):
                    scr[c] = z[:, c * LANES:(c + 1) * LANES]
                for r in range(d):
                    o_ref[0, r] = jnp.concatenate(
                        [scr[c, pl.ds(r, tm // d, stride=d), :] for c in range(nc)], axis=1).astype(BF16)
    gate = _sigmoid(jnp.dot(h, w_ref[0, :, 9 * gw:9 * gw + D], preferred_element_type=F32))
    gate_ref[0] = gate.astype(BF16)


def _qkv(x, sc, sh, w_c, cos_t, sup_t, sdn_t, layer):
    B, S, D = x.shape
    tm = MIXER_TILE
    gw = B_GROUP_WIDTH
    nw = w_c.shape[2]
    out_shapes, out_specs = [], []
    for d in DILATIONS:
        for _ in range(3):
            out_shapes.append(jax.ShapeDtypeStruct((B, d, S // d, gw), BF16))
            out_specs.append(pl.BlockSpec((1, d, tm // d, gw), lambda b, i: (b, 0, i, 0)))
    out_shapes.append(jax.ShapeDtypeStruct((B, S, D), BF16))
    out_specs.append(pl.BlockSpec((1, tm, D), lambda b, i: (b, i, 0)))
    return pl.pallas_call(
        _qkv_body,
        grid=(B, S // tm),
        in_specs=[
            pl.BlockSpec((1, tm, D), lambda b, i: (b, i, 0)),
            pl.BlockSpec((1, 1, D), lambda b, i: (b, 0, 0)),
            pl.BlockSpec((1, 1, D), lambda b, i: (b, 0, 0)),
            pl.BlockSpec((1, D, nw), lambda b, i: (layer, 0, 0), pipeline_mode=pl.Buffered(1)),
            pl.BlockSpec((tm, LANES), lambda b, i: (i, 0)),
            pl.BlockSpec((tm, LANES), lambda b, i: (i, 0)),
            pl.BlockSpec((tm, LANES), lambda b, i: (i, 0)),
        ],
        out_specs=out_specs,
        out_shape=out_shapes,
        scratch_shapes=[pltpu.VMEM((gw // LANES, tm, LANES), F32)],
        compiler_params=_params(("arbitrary", "arbitrary")),
        name="qkv",
    )(x, sc, sh, w_c, cos_t, sup_t, sdn_t)


def _attention_body(*refs):
    ins = refs[:15]
    o_ref = refs[15]
    o_scr, m_scr, l_scr, bias_scr, p_scr = refs[16:21]
    n = pl.program_id(1)
    blk = ATT_SPAN
    qi = lax.broadcasted_iota(I32, (blk, 2 * blk), 0)
    kj = lax.broadcasted_iota(I32, (blk, 2 * blk), 1)
    band = (kj >= qi) & (kj <= qi + ATT_SPAN)
    neg = jnp.float32(-jnp.inf)
    bias_scr[0] = jnp.where(band, 0.0, neg)
    bias_scr[1] = jnp.where(band & ((kj >= blk) | (n > 0)), 0.0, neg)
    lane = lax.broadcasted_iota(I32, (blk, LANES), 1)
    head0 = lane < HEAD_DIM
    ones = jnp.ones((2 * blk, LANES), BF16)

    def store(scr, g, d, r, j, val):
        view = scr.at[g, pl.ds(j * blk * d, blk * d)]
        if d == 1:
            view[...] = val
        else:
            view[pl.ds(r, blk, stride=d), :] = val

    def window(prev_ref, cur_ref, r, j):
        if j == 0:
            return jnp.concatenate([prev_ref[0, r], cur_ref[0, r, 0:blk]], axis=0)
        return cur_ref[0, r, (j - 1) * blk:(j + 1) * blk]

    chunk_id = 0
    for g, d in enumerate(DILATIONS):
        q_ref, kc_ref, kp_ref, vc_ref, vp_ref = ins[5 * g:5 * g + 5]
        nj = ATT_BLOCK // d // blk
        units = [(r, j) for r in range(d) for j in range(nj)]
        for c0 in range(0, len(units), ATT_CHUNK):
            chunk = units[c0:c0 + ATT_CHUNK]
            pbase = (chunk_id % 2) * 2 * ATT_CHUNK
            chunk_id += 1
            for ui, (r, j) in enumerate(chunk):
                q = q_ref[0, r, j * blk:(j + 1) * blk]
                kk = window(kp_ref, kc_ref, r, j)
                bias = bias_scr[1 if j == 0 else 0]
                zero = jnp.zeros_like(q)
                ms = []
                for h, qh in enumerate((jnp.where(head0, q, zero), jnp.where(head0, zero, q))):
                    s = lax.dot_general(qh, kk, (((1,), (1,)), ((), ())), preferred_element_type=F32) + bias
                    m = jnp.max(s, axis=1, keepdims=True)
                    p_scr[pbase + 2 * ui + h] = jnp.exp(s - m).astype(BF16)
                    ms.append(m)
                store(m_scr, g, d, r, j, jnp.where(head0, ms[0], ms[1]))
            for ui, (r, j) in enumerate(chunk):
                ve = jnp.concatenate([window(vp_ref, vc_ref, r, j), ones], axis=1)
                oe = [jnp.dot(p_scr[pbase + 2 * ui + h], ve, preferred_element_type=F32) for h in range(2)]
                store(o_scr, g, d, r, j, jnp.where(head0, oe[0][:, :LANES], oe[1][:, :LANES]))
                store(l_scr, g, d, r, j, jnp.where(head0, oe[0][:, LANES:], oe[1][:, LANES:]))

    def merge(c, carry):
        rows = pl.ds(pl.multiple_of(c * blk, blk), blk)
        ms = [m_scr[g, rows] for g in range(3)]
        m_ref = jnp.maximum(jnp.maximum(ms[0], ms[1]), ms[2])
        num = jnp.zeros((blk, LANES), F32)
        den = jnp.zeros((blk, LANES), F32)
        for g in range(3):
            w = jnp.exp(ms[g] - m_ref)
            num = num + w * o_scr[g, rows]
            den = den + w * l_scr[g, rows]
        o_ref[0, rows] = (num / den).astype(BF16)
        return carry
    lax.fori_loop(0, ATT_BLOCK // blk, merge, 0)


def _attention(qkv):
    B = qkv[0].shape[0]
    S = qkv[0].shape[2]
    gw = B_GROUP_WIDTH
    blk = ATT_SPAN
    nblk = S // ATT_BLOCK
    in_specs, args = [], []
    for g, d in enumerate(DILATIONS):
        rows = ATT_BLOCK // d
        per = rows // blk
        cur = pl.BlockSpec((1, d, rows, LANES), lambda b, n, hp: (b, 0, n, hp))
        prev = pl.BlockSpec((1, d, blk, LANES),
                            lambda b, n, hp, per=per: (b, 0, jnp.maximum(n * per - 1, 0), hp))
        q, k, v = qkv[3 * g:3 * g + 3]
        in_specs += [cur, cur, prev, cur, prev]
        args += [q, k, k, v, v]
    return pl.pallas_call(
        _attention_body,
        grid=(B, nblk, gw // LANES),
        in_specs=in_specs,
        out_specs=pl.BlockSpec((1, ATT_BLOCK, LANES), lambda b, n, hp: (b, n, hp)),
        out_shape=jax.ShapeDtypeStruct((B, S, gw), BF16),
        scratch_shapes=[pltpu.VMEM((3, ATT_BLOCK, LANES), F32)] * 3 + [
            pltpu.VMEM((2, blk, 2 * blk), F32),
            pltpu.VMEM((4 * ATT_CHUNK, blk, 2 * blk), BF16)],
        compiler_params=_params(("arbitrary", "arbitrary", "arbitrary")),
        name="attention",
    )(*args)


def _post_mixer_body(alpha, x_ref, ya_ref, gb_ref, ob_ref, wbb_ref, wout_ref, gm_ref, lng_ref, lnb_ref,
                     sc_ref, sh_ref, wr_ref, br_ref,
                     x1_ref, h2_ref, pos_ref, wts_ref, cnt_ref):
    tt = x_ref.shape[1]
    yb = jnp.dot(ob_ref[0], wbb_ref[0], preferred_element_type=F32)
    merged = (ya_ref[0].astype(F32) + gb_ref[0].astype(F32) * yb).astype(BF16)
    y = jnp.dot(merged, wout_ref[0], preferred_element_type=F32)
    x1 = _layer_norm(alpha * x_ref[0] + (1.0 + gm_ref[0]) * y, lng_ref[0], lnb_ref[0])
    x1_ref[0] = x1
    h2 = x1 * (1.0 + sc_ref[0]) + sh_ref[0]
    h2_ref[...] = h2.astype(BF16)

    logits = lax.dot_general(wr_ref[0], h2, (((1,), (1,)), ((), ())), preferred_element_type=F32,
                             precision=lax.Precision.HIGHEST) + br_ref[0]
    eid = lax.broadcasted_iota(I32, (N_EXPERTS, tt), 0)
    vals, hots = [], []
    cur = logits
    for k in range(TOP_K):
        m = jnp.max(cur, axis=0, keepdims=True)
        sel = jnp.min(jnp.where(cur == m, eid, N_EXPERTS), axis=0, keepdims=True)
        hot = eid == sel
        vals.append(m)
        hots.append(hot)
        cur = jnp.where(hot, -jnp.inf, cur)
    ex = [jnp.exp(v - vals[0]) for v in vals]
    tot = ex[0] + ex[1] + ex[2] + ex[3]
    for k in range(TOP_K):
        wts_ref[k:k + 1, :] = ex[k] / tot

    member = jnp.zeros((N_EXPERTS, tt), F32)
    for hot in hots:
        member = member + hot.astype(F32)
    before = (lax.broadcasted_iota(I32, (tt, tt), 0) < lax.broadcasted_iota(I32, (tt, tt), 1))
    excl = jnp.dot(member.astype(BF16), before.astype(BF16), preferred_element_type=F32)
    cnt = jnp.broadcast_to(jnp.sum(member, axis=1, keepdims=True), (N_EXPERTS, LANES))
    cnt_ref[...] = cnt
    lower = (lax.broadcasted_iota(I32, (N_EXPERTS, N_EXPERTS), 1)
             < lax.broadcasted_iota(I32, (N_EXPERTS, N_EXPERTS), 0)).astype(F32)
    cnt_al = jnp.ceil(cnt * (1.0 / RUN_ALIGN)) * RUN_ALIGN
    off = jnp.dot(lower, cnt_al, preferred_element_type=F32, precision=lax.Precision.HIGHEST)[:, 0:1]
    base = off + excl
    for k in range(TOP_K):
        pos_ref[k:k + 1, :] = jnp.sum(jnp.where(hots[k], base, 0.0), axis=0, keepdims=True).astype(I32)


def _post_mixer(x, ya, gb, ob, w_bb, w_out, gm, ln_g, ln_b, sc, sh, w_rt, b_r, layer, depth):
    B, S, D = x.shape
    tt = TOKEN_TILE
    nt = S // tt
    T = B * S
    gw = B_GROUP_WIDTH
    tok = lambda b, i: (b, i, 0)
    per_b = lambda b, i: (b, 0, 0)
    per_l3 = lambda b, i: (layer, 0, 0)
    flat = lambda b, i: (0, b * nt + i)
    return pl.pallas_call(
        functools.partial(_post_mixer_body, (2.0 * depth) ** 0.25),
        grid=(B, nt),
        in_specs=[
            pl.BlockSpec((1, tt, D), tok),
            pl.BlockSpec((1, tt, D), tok),
            pl.BlockSpec((1, tt, D), tok),
            pl.BlockSpec((1, tt, gw), tok),
            pl.BlockSpec((1, gw, D), per_l3),
            pl.BlockSpec((1, D, D), per_l3),
            pl.BlockSpec((1, 1, D), per_b),
            pl.BlockSpec((1, 1, D), per_l3),
            pl.BlockSpec((1, 1, D), per_l3),
            pl.BlockSpec((1, 1, D), per_b),
            pl.BlockSpec((1, 1, D), per_b),
            pl.BlockSpec((1, N_EXPERTS, D), per_l3),
            pl.BlockSpec((1, N_EXPERTS, 1), per_l3),
        ],
        out_specs=[
            pl.BlockSpec((1, tt, D), tok),
            pl.BlockSpec((tt, D), lambda b, i: (b * nt + i, 0)),
            pl.BlockSpec((TOP_K, tt), flat),
            pl.BlockSpec((TOP_K, tt), flat),
            pl.BlockSpec((N_EXPERTS, LANES), lambda b, i: (b * nt + i, 0)),
        ],
        out_shape=[
            jax.ShapeDtypeStruct((B, S, D), F32),
            jax.ShapeDtypeStruct((T, D), BF16),
            jax.ShapeDtypeStruct((TOP_K, T), I32),
            jax.ShapeDtypeStruct((TOP_K, T), F32),
            jax.ShapeDtypeStruct((B * nt * N_EXPERTS, LANES), F32),
        ],
        compiler_params=_params(("arbitrary", "arbitrary")),
        name="post_mixer",
    )(x, ya, gb, ob, w_bb, w_out, gm, ln_g, ln_b, sc, sh, w_rt, b_r)


def _run_pieces(n, fn):
    for bit in reversed(range(RUN_ALIGN.bit_length() - 1, RUN_BITS)):
        size = 1 << bit
        done = pl.multiple_of((n >> (bit + 1)) << (bit + 1), RUN_ALIGN)

        @pl.when((n & size) != 0)
        def _():
            fn(done, size)


def _each_run(step, len_ref, a_off_ref, b_off_ref, make_copy, action):
    def run(e, carry):
        i = step * N_EXPERTS + e
        a0, b0 = a_off_ref[i], b_off_ref[i]
        _run_pieces(len_ref[i], lambda off, size: action(make_copy(
            pl.multiple_of(a0 + off, RUN_ALIGN), pl.multiple_of(b0 + off, RUN_ALIGN), size)))
        return carry
    lax.fori_loop(0, N_EXPERTS, run, 0)


def _dispatch_body(tile_off_ref, slot_off_ref, len_ref, ends_ref, padded_ref, h_ref, pos_ref, xs_ref,
                   zbuf, rbuf, fill_sem, sem):
    t = pl.program_id(0)
    last = pl.num_programs(0) - 1
    tm = zbuf.shape[0]
    tt = h_ref.shape[0]

    @pl.when(t == 0)
    def _():
        zbuf[...] = jnp.zeros_like(zbuf)
        for e in range(N_EXPERTS):
            @pl.when(padded_ref[e] > 0)
            def _():
                start = pl.multiple_of(ends_ref[e] - tm, tm)
                cp = pltpu.make_async_copy(zbuf, xs_ref.at[pl.ds(start, tm)], fill_sem)
                cp.start()
                cp.wait()

        def fill_unused(i, carry):
            cp = pltpu.make_async_copy(zbuf, xs_ref.at[pl.ds(pl.multiple_of(i * tm, tm), tm)], fill_sem)
            cp.start()
            cp.wait()
            return carry
        lax.fori_loop(ends_ref[N_EXPERTS - 1] // tm, xs_ref.shape[0] // tm, fill_unused, 0)

    buf = t % 2
    h = h_ref[...]
    pos = [pos_ref[k:k + 1, :] for k in range(TOP_K)]
    for c in range(rbuf.shape[1] // PERM_ROWS):
        p_id = lax.broadcasted_iota(I32, (PERM_ROWS, tt), 0) + c * PERM_ROWS
        sel = (p_id == pos[0]) | (p_id == pos[1]) | (p_id == pos[2]) | (p_id == pos[3])
        rows = jnp.dot(jnp.where(sel, 1.0, 0.0).astype(BF16), h, preferred_element_type=F32)
        rbuf[buf, c * PERM_ROWS:(c + 1) * PERM_ROWS, :] = _pack_bf16_pairs(rows)

    def runs(s, action):
        def make_copy(tile_row, slot_row, size):
            return pltpu.make_async_copy(rbuf.at[s % 2, pl.ds(tile_row, size)], xs_ref.at[pl.ds(slot_row, size)],
                                         sem.at[s % 2])
        _each_run(s, len_ref, tile_off_ref, slot_off_ref, make_copy, action)

    runs(t, lambda cp: cp.start())

    @pl.when(t > 0)
    def _():
        runs(t - 1, lambda cp: cp.wait())

    @pl.when(t == last)
    def _():
        runs(t, lambda cp: cp.wait())


def _dispatch(h2, pos, tile_off, slot_off, run_len, ends, padded, n_slots):
    T, D = h2.shape
    tt = TOKEN_TILE
    grid_spec = pltpu.PrefetchScalarGridSpec(
        num_scalar_prefetch=5,
        grid=(T // tt,),
        in_specs=[pl.BlockSpec((tt, D), lambda t, *_: (t, 0)),
                  pl.BlockSpec((TOP_K, tt), lambda t, *_: (0, t))],
        out_specs=pl.BlockSpec(memory_space=pl.ANY),
        scratch_shapes=[pltpu.VMEM((EXPERT_TILE, D // 2), U32), pltpu.VMEM((2, RUN_TILE_ROWS, D // 2), U32),
                        pltpu.SemaphoreType.DMA, pltpu.SemaphoreType.DMA((2,))],
    )
    return pl.pallas_call(
        _dispatch_body,
        grid_spec=grid_spec,
        out_shape=jax.ShapeDtypeStruct((n_slots, D // 2), U32),
        compiler_params=pltpu.CompilerParams(dimension_semantics=("arbitrary",), vmem_limit_bytes=VMEM_LIMIT,
                                             has_side_effects=True),
        name="dispatch",
    )(tile_off, slot_off, run_len, ends, padded, h2, pos)


def _experts_body(layer, te_ref, nact_ref, next_ref, xs_ref, wup_hbm, bup_ref, wdn_hbm, bdn_ref, ys_ref,
                  wup_f, wdn_f, wup_s, wdn_s, sem):
    i = pl.program_id(0)
    ff = wdn_s.shape[0]
    e = te_ref[i]

    def fetch(expert):
        row = layer * N_EXPERTS + expert
        return (pltpu.make_async_copy(wup_hbm.at[row], wup_f, sem.at[0]),
                pltpu.make_async_copy(wdn_hbm.at[row], wdn_f, sem.at[1]))

    @pl.when(i == 0)
    def _():
        for cp in fetch(e):
            cp.start()

    @pl.when((i == 0) | (e != te_ref[jnp.maximum(i - 1, 0)]))
    def _():
        for cp in fetch(e):
            cp.wait()
        n = 2 * LANES
        src = lax.broadcasted_iota(I32, (n, n), 0)
        dst = lax.broadcasted_iota(I32, (n, n), 1)
        perm = (src == jnp.where(dst < LANES, 2 * dst, 2 * (dst - LANES) + 1)).astype(BF16)
        for blk in range(wup_f.shape[1] // n):
            cols = slice(blk * n, (blk + 1) * n)
            wup_s[:, cols] = jnp.dot(wup_f[:, cols].astype(BF16), perm, preferred_element_type=F32).astype(BF16)
        wdn_s[...] = wdn_f[...].astype(BF16)

        @pl.when(next_ref[e] >= 0)
        def _():
            for cp in fetch(next_ref[e]):
                cp.start()

    @pl.when(i < nact_ref[0])
    def _():
        wd = xs_ref[...]
        x = jnp.concatenate([_unpack_hi(wd), _unpack_lo(wd)], axis=1).astype(BF16)
        hh = jnp.dot(x, wup_s[...], preferred_element_type=F32) + bup_ref[0]
        acts = []
        for blk in range(ff // LANES):
            glu = jnp.minimum(hh[:, 2 * blk * LANES:(2 * blk + 1) * LANES], SWIGLU_LIMIT)
            lin = jnp.clip(hh[:, (2 * blk + 1) * LANES:(2 * blk + 2) * LANES], -SWIGLU_LIMIT, SWIGLU_LIMIT)
            acts.append((glu * _sigmoid(SWIGLU_ALPHA * glu) * (lin + 1.0)).astype(BF16))
        act = jnp.concatenate(acts, axis=1)
        y = jnp.dot(act, wdn_s[...], preferred_element_type=F32) + bdn_ref[0]
        ys_ref[...] = _pack_bf16_pairs(y)

    @pl.when(i >= nact_ref[0])
    def _():
        ys_ref[...] = jnp.zeros_like(ys_ref)


def _experts(xs, tile_expert, n_active, next_expert, w_up, b_up, w_dn, b_dn, layer):
    P, W = xs.shape
    tm = EXPERT_TILE
    D = 2 * W
    ff2 = w_up.shape[2]
    ff = w_dn.shape[1]
    ex = lambda i, te, na, nx: (layer * N_EXPERTS + te[i], 0, 0)
    grid_spec = pltpu.PrefetchScalarGridSpec(
        num_scalar_prefetch=3,
        grid=(P // tm,),
        in_specs=[
            pl.BlockSpec((tm, W), lambda i, te, na, nx: (jnp.where(i < na[0], i, 0), 0)),
            pl.BlockSpec(memory_space=pl.ANY),
            pl.BlockSpec((1, 1, ff2), ex),
            pl.BlockSpec(memory_space=pl.ANY),
            pl.BlockSpec((1, 1, D), ex),
        ],
        out_specs=pl.BlockSpec((tm, W), lambda i, te, na, nx: (i, 0)),
        scratch_shapes=[pltpu.VMEM((D, ff2), F32), pltpu.VMEM((ff, D), F32),
                        pltpu.VMEM((D, ff2), BF16), pltpu.VMEM((ff, D), BF16), pltpu.SemaphoreType.DMA((2,))],
    )
    return pl.pallas_call(
        functools.partial(_experts_body, layer),
        grid_spec=grid_spec,
        out_shape=jax.ShapeDtypeStruct((P, W), U32),
        compiler_params=_params(("arbitrary",)),
        name="experts",
    )(tile_expert, n_active, next_expert, xs, w_up, b_up, w_dn, b_dn)


def _combine_body(alpha, tile_off_ref, slot_off_ref, len_ref, x_ref, ys_ref, pos_ref, wts_ref, gm_ref, lng_ref,
                  lnb_ref, o_ref, ybuf, sem):
    tt = x_ref.shape[1]
    n = ybuf.shape[1]
    step = pl.program_id(0) * pl.num_programs(1) + pl.program_id(1)
    n_steps = pl.num_programs(0) * pl.num_programs(1)

    def runs(s, action):
        buf = s % 2

        def make_copy(tile_row, slot_row, size):
            return pltpu.make_async_copy(ys_ref.at[pl.ds(slot_row, size)], ybuf.at[buf, pl.ds(tile_row, size)],
                                         sem.at[buf])
        _each_run(s, len_ref, tile_off_ref, slot_off_ref, make_copy, action)

    def fetch(s):
        ybuf[s % 2, TOP_K * tt:n, :] = jnp.zeros((n - TOP_K * tt, ybuf.shape[2]), U32)
        runs(s, lambda cp: cp.start())

    @pl.when(step == 0)
    def _():
        fetch(step)

    @pl.when(step + 1 < n_steps)
    def _():
        fetch(step + 1)

    runs(step, lambda cp: cp.wait())

    p_id = lax.broadcasted_iota(I32, (tt, n), 1)
    a = jnp.zeros((tt, n), F32)
    for k in reversed(range(TOP_K)):
        a = jnp.where(p_id == pos_ref[:, k:k + 1], wts_ref[:, k:k + 1], a)
    a = a.astype(BF16)
    wd = ybuf[step % 2]
    hi = jnp.dot(a, _unpack_hi(wd).astype(BF16), preferred_element_type=F32)
    lo = jnp.dot(a, _unpack_lo(wd).astype(BF16), preferred_element_type=F32)
    y = jnp.concatenate([hi, lo], axis=1)
    o_ref[0] = _layer_norm(alpha * x_ref[0] + (1.0 + gm_ref[0]) * y, lng_ref[0], lnb_ref[0])


def _combine(x1, ys, tile_off, slot_off, run_len, pos_t, wts_t, gm, ln_g, ln_b, layer, depth):
    B, S, D = x1.shape
    tt = TOKEN_TILE
    nt = S // tt
    grid_spec = pltpu.PrefetchScalarGridSpec(
        num_scalar_prefetch=3,
        grid=(B, nt),
        in_specs=[
            pl.BlockSpec((1, tt, D), lambda b, i, *_: (b, i, 0)),
            pl.BlockSpec(memory_space=pl.ANY),
            pl.BlockSpec((tt, TOP_K), lambda b, i, *_: (b * nt + i, 0)),
            pl.BlockSpec((tt, TOP_K), lambda b, i, *_: (b * nt + i, 0)),
            pl.BlockSpec((1, 1, D), lambda b, i, *_: (b, 0, 0)),
            pl.BlockSpec((1, 1, D), lambda b, i, *_: (layer, 0, 0)),
            pl.BlockSpec((1, 1, D), lambda b, i, *_: (layer, 0, 0)),
        ],
        out_specs=pl.BlockSpec((1, tt, D), lambda b, i, *_: (b, i, 0)),
        scratch_shapes=[pltpu.VMEM((2, RUN_TILE_ROWS, D // 2), U32), pltpu.SemaphoreType.DMA((2,))],
    )
    return pl.pallas_call(
        functools.partial(_combine_body, (2.0 * depth) ** 0.25),
        grid_spec=grid_spec,
        out_shape=jax.ShapeDtypeStruct((B, S, D), F32),
        compiler_params=_params(("arbitrary", "arbitrary")),
        name="combine",
    )(tile_off, slot_off, run_len, x1, ys, pos_t, wts_t, gm, ln_g, ln_b)


def _rope_tables(S):
    half = ROPE_DIM // 2
    inv = ROPE_THETA ** (-jnp.arange(half, dtype=F32) / half)
    ang = jnp.arange(S, dtype=F32)[:, None] * inv[None, :]
    cos, sin = jnp.cos(ang), jnp.sin(ang)
    lane = jnp.arange(LANES) % HEAD_DIM
    pick = lane % half
    cos_l, sin_l = cos[:, pick], sin[:, pick]
    cos_t = jnp.where(lane < ROPE_DIM, cos_l, 1.0)
    s_up = jnp.where((lane >= half) & (lane < ROPE_DIM), sin_l, 0.0)
    s_dn = jnp.where(lane < half, -sin_l, 0.0)
    return cos_t.astype(F32), s_up.astype(F32), s_dn.astype(F32)


def kernel(x, c, w_ada, b_ada, w_in, spatial_w, spatial_b, w_branch_a, w_branch_b, w_out, ln1_g, ln1_b,
           w_router, b_router, w_up, b_up, w_down, b_down, ln2_g, ln2_b):
    B, S, D = x.shape
    L = w_ada.shape[0]
    T = B * S
    gw = B_GROUP_WIDTH
    nq = len(DILATIONS) * gw
    assert S % ATT_BLOCK == 0 and S % TOKEN_TILE == 0 and D % (2 * LANES) == 0

    q0, k0, v0, g0 = 2 * D, 2 * D + nq, 2 * D + 2 * nq, 2 * D + 3 * nq
    w_a = jnp.concatenate([w_in[:, :, :2 * D], w_in[:, :, g0:g0 + D]], axis=2).astype(BF16)
    cols = []
    for g in range(len(DILATIONS)):
        for base in (q0, k0, v0):
            cols.append(w_in[:, :, base + g * gw:base + (g + 1) * gw])
    cols.append(w_in[:, :, g0 + D:g0 + 2 * D])
    w_c = jnp.concatenate(cols, axis=2).astype(BF16)
    w_ba = w_branch_a.astype(BF16)
    w_bb = w_branch_b.astype(BF16)
    w_o = w_out.astype(BF16)
    sp_b = spatial_b[..., None]
    w_rt = jnp.swapaxes(w_router, 1, 2)
    b_r = b_router[..., None]
    ff = w_down.shape[2]
    w_up_p = w_up.reshape(L * N_EXPERTS, D, 2 * ff)
    b_up_p = jnp.swapaxes(b_up.reshape(L * N_EXPERTS, 1, ff // LANES, LANES, 2), 3, 4).reshape(
        L * N_EXPERTS, 1, 2 * ff)
    w_dn_p = w_down.reshape(L * N_EXPERTS, ff, D)
    b_dn_p = b_down.reshape(L * N_EXPERTS, 1, D)
    ln1g, ln1b = ln1_g[:, None, :], ln1_b[:, None, :]
    ln2g, ln2b = ln2_g[:, None, :], ln2_b[:, None, :]
    cos_t, sup_t, sdn_t = _rope_tables(S)

    mod = _ada_mod(c, w_ada, b_ada)
    n_slots = T * TOP_K + (T // TOKEN_TILE) * N_EXPERTS * RUN_ALIGN + N_EXPERTS * EXPERT_TILE
    n_tiles = n_slots // EXPERT_TILE

    for l in range(L):
        sh1, sc1, g1, sh2, sc2, g2 = [mod[l, :, None, i * D:(i + 1) * D] for i in range(6)]
        ya = _branch_a(x, sc1, sh1, w_a, spatial_w, sp_b, w_ba, l)
        *qkv, gb = _qkv(x, sc1, sh1, w_c, cos_t, sup_t, sdn_t, l)
        ob = _attention(qkv)
        x1, h2, pos, wts, cnt = _post_mixer(x, ya, gb, ob, w_bb, w_o, g1, ln1g, ln1b, sc2, sh2, w_rt, b_r, l, L)
        n_tok_tiles = T // TOKEN_TILE
        tc = cnt.reshape(n_tok_tiles, N_EXPERTS, LANES)[:, :, 0].astype(I32)
        tc = ((tc + RUN_ALIGN - 1) // RUN_ALIGN) * RUN_ALIGN
        off_in_tile = jnp.cumsum(tc, axis=1) - tc
        seen_before = jnp.cumsum(tc, axis=0) - tc
        counts = jnp.sum(tc, axis=0)
        padded = ((counts + EXPERT_TILE - 1) // EXPERT_TILE) * EXPERT_TILE
        ends = jnp.cumsum(padded).astype(I32)
        starts = ends - padded
        tile_off = off_in_tile.astype(I32).reshape(-1)
        slot_off = (starts[None, :] + seen_before).astype(I32).reshape(-1)
        run_len = tc.reshape(-1)
        n_active = (ends[-1] // EXPERT_TILE).astype(I32)
        tile_start = jnp.arange(n_tiles, dtype=I32) * EXPERT_TILE
        te = jnp.sum((ends[None, :] <= tile_start[:, None]).astype(I32), axis=1)
        te_last = jnp.sum((ends <= (n_active - 1) * EXPERT_TILE).astype(I32))
        te = jnp.minimum(jnp.where(tile_start < ends[-1], te, te_last), N_EXPERTS - 1).astype(I32)

        padded = padded.astype(I32)
        xs = _dispatch(h2, pos, tile_off, slot_off, run_len, ends, padded, n_slots)
        eid = jnp.arange(N_EXPERTS, dtype=I32)
        later = (eid[None, :] > eid[:, None]) & (padded[None, :] > 0)
        next_expert = jnp.min(jnp.where(later, eid[None, :], N_EXPERTS), axis=1)
        next_expert = jnp.where(next_expert < N_EXPERTS, next_expert, -1).astype(I32)
        ys = _experts(xs, te, n_active.reshape(1), next_expert, w_up_p, b_up_p, w_dn_p, b_dn_p, l)
        x = _combine(x1, ys, tile_off, slot_off, run_len, pos.T, wts.T, g2, ln2g, ln2b, l, L)
    return x
```

```python
import functools
import math

import jax
import jax.numpy as jnp
from jax import lax
from jax.experimental import pallas as pl
from jax.experimental.pallas import tpu as pltpu

BF16 = jnp.bfloat16
F32 = jnp.float32
U32 = jnp.uint32
I32 = jnp.int32

A_CHUNK = 128
A_GROUPS = 4
HEAD_DIM = 64
B_HEADS = 8
DILATIONS = (1, 4, 16)
ATT_SPAN = 128
B_GROUP_WIDTH = B_HEADS * HEAD_DIM
ROPE_DIM = HEAD_DIM // 4
ROPE_THETA = 500000.0
N_EXPERTS = 32
TOP_K = 4
SWIGLU_LIMIT = 7.0
SWIGLU_ALPHA = 1.702
LN_EPS = 1e-5

LANES = 128
VMEM_LIMIT = 56 * 1024 * 1024
TOKEN_TILE = 512
MIXER_TILE = 1024
ATT_BLOCK = 128 * DILATIONS[-1]
EXPERT_TILE = 512
ATT_CHUNK = 4
RUN_BITS = TOKEN_TILE.bit_length()
RUN_ALIGN = 8
RUN_TILE_ROWS = TOP_K * TOKEN_TILE + N_EXPERTS * RUN_ALIGN
PERM_ROWS = 256


def _gelu(x):
    return 0.5 * x * (1.0 + jnp.tanh(math.sqrt(2.0 / math.pi) * (x + 0.044715 * (x * x * x))))


def _sigmoid(x):
    return 1.0 / (1.0 + jnp.exp(-x))


def _layer_norm(r, g, b):
    mu = jnp.mean(r, axis=-1, keepdims=True)
    c = r - mu
    var = jnp.mean(c * c, axis=-1, keepdims=True)
    return c * lax.rsqrt(var + LN_EPS) * g + b


def _pack_bf16_pairs(x):
    w = x.shape[1] // 2
    bits = lax.bitcast_convert_type(x.astype(BF16).astype(F32), U32)
    return bits[:, :w] | lax.shift_right_logical(bits[:, w:], jnp.uint32(16))


def _unpack_hi(wd):
    return lax.bitcast_convert_type(wd & jnp.uint32(0xFFFF0000), F32)


def _unpack_lo(wd):
    return lax.bitcast_convert_type(lax.shift_left(wd, jnp.uint32(16)), F32)


def _params(sem):
    return pltpu.CompilerParams(dimension_semantics=sem, vmem_limit_bytes=VMEM_LIMIT)


def _ada_body(c_ref, w_ref, b_ref, o_ref):
    o_ref[0] = jnp.dot(c_ref[...], w_ref[0], preferred_element_type=F32,
                       precision=lax.Precision.HIGHEST) + b_ref[0]


def _ada_mod(c, w_ada, b_ada):
    L, D, N = w_ada.shape
    B = c.shape[0]
    tn = D
    return pl.pallas_call(
        _ada_body,
        grid=(L, N // tn),
        in_specs=[
            pl.BlockSpec((B, D), lambda l, j: (0, 0)),
            pl.BlockSpec((1, D, tn), lambda l, j: (l, 0, j)),
            pl.BlockSpec((1, 1, tn), lambda l, j: (l, 0, j)),
        ],
        out_specs=pl.BlockSpec((1, B, tn), lambda l, j: (l, 0, j)),
        out_shape=jax.ShapeDtypeStruct((L, B, N), F32),
        compiler_params=_params(("arbitrary", "arbitrary")),
        name="ada_mod",
    )(c, w_ada, b_ada.reshape(L, 1, N))


def _branch_a_body(x_ref, sc_ref, sh_ref, wa_ref, spw_ref, spb_ref, wba_ref, o_ref, a_scr):
    tm, D = x_ref.shape[1], x_ref.shape[2]
    gd = D // A_GROUPS
    h = (x_ref[0] * (1.0 + sc_ref[0]) + sh_ref[0]).astype(BF16)
    row = lax.broadcasted_iota(I32, (A_CHUNK, A_CHUNK), 0)
    col = lax.broadcasted_iota(I32, (A_CHUNK, A_CHUNK), 1)
    causal = row >= col
    for g in range(A_GROUPS):
        v = _gelu(jnp.dot(h, wa_ref[0, :, D + g * gd:D + (g + 1) * gd], preferred_element_type=F32))
        mu = jnp.mean(v, axis=-1, keepdims=True)
        vc = v - mu
        var = jnp.mean(vc * vc, axis=-1, keepdims=True)
        vn = (vc * lax.rsqrt(var + LN_EPS)).astype(BF16)
        u = _gelu(jnp.dot(h, wa_ref[0, :, g * gd:(g + 1) * gd], preferred_element_type=F32))
        ws = jnp.where(causal, spw_ref[0, g], 0.0).astype(BF16)
        bias = spb_ref[0, g]
        for c in range(tm // A_CHUNK):
            rows = slice(c * A_CHUNK, (c + 1) * A_CHUNK)
            sv = jnp.dot(ws, vn[rows], preferred_element_type=F32) + bias
            a_scr[rows, g * gd:(g + 1) * gd] = (u[rows] * sv).astype(BF16)
    ya = jnp.dot(a_scr[...], wba_ref[0], preferred_element_type=F32)
    gate = _sigmoid(jnp.dot(h, wa_ref[0, :, 2 * D:3 * D], preferred_element_type=F32))
    o_ref[0] = (gate * ya).astype(BF16)


def _branch_a(x, sc, sh, w_a, sp_w, sp_b, w_ba, layer):
    B, S, D = x.shape
    tm = MIXER_TILE
    return pl.pallas_call(
        _branch_a_body,
        grid=(B, S // tm),
        in_specs=[
            pl.BlockSpec((1, tm, D), lambda b, i: (b, i, 0)),
            pl.BlockSpec((1, 1, D), lambda b, i: (b, 0, 0)),
            pl.BlockSpec((1, 1, D), lambda b, i: (b, 0, 0)),
            pl.BlockSpec((1, D, 3 * D), lambda b, i: (layer, 0, 0)),
            pl.BlockSpec((1, A_GROUPS, A_CHUNK, A_CHUNK), lambda b, i: (layer, 0, 0, 0)),
            pl.BlockSpec((1, A_GROUPS, A_CHUNK, 1), lambda b, i: (layer, 0, 0, 0)),
            pl.BlockSpec((1, D, D), lambda b, i: (layer, 0, 0)),
        ],
        out_specs=pl.BlockSpec((1, tm, D), lambda b, i: (b, i, 0)),
        out_shape=jax.ShapeDtypeStruct((B, S, D), BF16),
        scratch_shapes=[pltpu.VMEM((tm, D), BF16)],
        compiler_params=_params(("arbitrary", "arbitrary")),
        name="branch_a",
    )(x, sc, sh, w_a, sp_w, sp_b, w_ba)


def _rope(z, cos, s_up, s_dn):
    half = ROPE_DIM // 2
    parts = []
    for c in range(z.shape[1] // LANES):
        zc = z[:, c * LANES:(c + 1) * LANES]
        parts.append(zc * cos + pltpu.roll(zc, half, 1) * s_up + pltpu.roll(zc, LANES - half, 1) * s_dn)
    return jnp.concatenate(parts, axis=1)


def _qkv_body(x_ref, sc_ref, sh_ref, w_ref, cos_ref, sup_ref, sdn_ref, *rest):
    outs = rest[:9]
    gate_ref = rest[9]
    scr = rest[10]
    tm, D = x_ref.shape[1], x_ref.shape[2]
    gw = B_GROUP_WIDTH
    h = (x_ref[0] * (1.0 + sc_ref[0]) + sh_ref[0]).astype(BF16)
    cos, s_up, s_dn = cos_ref[...], sup_ref[...], sdn_ref[...]
    for g, d in enumerate(DILATIONS):
        for part in range(3):
            col = (3 * g + part) * gw
            z = jnp.dot(h, w_ref[0, :, col:col + gw], preferred_element_type=F32)
            if part < 2:
                z = _rope(z, cos, s_up, s_dn)
            if part == 0:
                z = z * (HEAD_DIM ** -0.5)
            o_ref = outs[3 * g + part]
            if d == 1:
                o_ref[0, 0] = z.astype(BF16)
            else:
                nc = gw // LANES
                for c in range(nc):
                    scr[c] = z[:, c * LANES:(c + 1) * LANES]
                for r in range(d):
                    o_ref[0, r] = jnp.concatenate(
                        [scr[c, pl.ds(r, tm // d, stride=d), :] for c in range(nc)], axis=1).astype(BF16)
    gate = _sigmoid(jnp.dot(h, w_ref[0, :, 9 * gw:9 * gw + D], preferred_element_type=F32))
    gate_ref[0] = gate.astype(BF16)


def _qkv(x, sc, sh, w_c, cos_t, sup_t, sdn_t, layer):
    B, S, D = x.shape
    tm = MIXER_TILE
    gw = B_GROUP_WIDTH
    nw = w_c.shape[2]
    out_shapes, out_specs = [], []
    for d in DILATIONS:
        for _ in range(3):
            out_shapes.append(jax.ShapeDtypeStruct((B, d, S // d, gw), BF16))
            out_specs.append(pl.BlockSpec((1, d, tm // d, gw), lambda b, i: (b, 0, i, 0)))
    out_shapes.append(jax.ShapeDtypeStruct((B, S, D), BF16))
    out_specs.append(pl.BlockSpec((1, tm, D), lambda b, i: (b, i, 0)))
    return pl.pallas_call(
        _qkv_body,
        grid=(B, S // tm),
        in_specs=[
            pl.BlockSpec((1, tm, D), lambda b, i: (b, i, 0)),
            pl.BlockSpec((1, 1, D), lambda b, i: (b, 0, 0)),
            pl.BlockSpec((1, 1, D), lambda b, i: (b, 0, 0)),
            pl.BlockSpec((1, D, nw), lambda b, i: (layer, 0, 0), pipeline_mode=pl.Buffered(1)),
            pl.BlockSpec((tm, LANES), lambda b, i: (i, 0)),
            pl.BlockSpec((tm, LANES), lambda b, i: (i, 0)),
            pl.BlockSpec((tm, LANES), lambda b, i: (i, 0)),
        ],
        out_specs=out_specs,
        out_shape=out_shapes,
        scratch_shapes=[pltpu.VMEM((gw // LANES, tm, LANES), F32)],
        compiler_params=_params(("arbitrary", "arbitrary")),
        name="qkv",
    )(x, sc, sh, w_c, cos_t, sup_t, sdn_t)


def _attention_body(*refs):
    ins = refs[:15]
    o_ref = refs[15]
    o_scr, m_scr, l_scr, bias_scr, p_scr = refs[16:21]
    n = pl.program_id(1)
    blk = ATT_SPAN
    qi = lax.broadcasted_iota(I32, (blk, 2 * blk), 0)
    kj = lax.broadcasted_iota(I32, (blk, 2 * blk), 1)
    band = (kj >= qi) & (kj <= qi + ATT_SPAN)
    neg = jnp.float32(-jnp.inf)
    bias_scr[0] = jnp.where(band, 0.0, neg)
    bias_scr[1] = jnp.where(band & ((kj >= blk) | (n > 0)), 0.0, neg)
    lane = lax.broadcasted_iota(I32, (blk, LANES), 1)
    head0 = lane < HEAD_DIM
    ones = jnp.ones((2 * blk, LANES), BF16)

    def store(scr, g, d, r, j, val):
        view = scr.at[g, pl.ds(j * blk * d, blk * d)]
        if d == 1:
            view[...] = val
        else:
            view[pl.ds(r, blk, stride=d), :] = val

    def window(prev_ref, cur_ref, r, j):
        if j == 0:
            return jnp.concatenate([prev_ref[0, r], cur_ref[0, r, 0:blk]], axis=0)
        return cur_ref[0, r, (j - 1) * blk:(j + 1) * blk]

    chunk_id = 0
    for g, d in enumerate(DILATIONS):
        q_ref, kc_ref, kp_ref, vc_ref, vp_ref = ins[5 * g:5 * g + 5]
        nj = ATT_BLOCK // d // blk
        units = [(r, j) for r in range(d) for j in range(nj)]
        for c0 in range(0, len(units), ATT_CHUNK):
            chunk = units[c0:c0 + ATT_CHUNK]
            pbase = (chunk_id % 2) * 2 * ATT_CHUNK
            chunk_id += 1
            for ui, (r, j) in enumerate(chunk):
                q = q_ref[0, r, j * blk:(j + 1) * blk]
                kk = window(kp_ref, kc_ref, r, j)
                bias = bias_scr[1 if j == 0 else 0]
                zero = jnp.zeros_like(q)
                ms = []
                for h, qh in enumerate((jnp.where(head0, q, zero), jnp.where(head0, zero, q))):
                    s = lax.dot_general(qh, kk, (((1,), (1,)), ((), ())), preferred_element_type=F32) + bias
                    m = jnp.max(s, axis=1, keepdims=True)
                    p_scr[pbase + 2 * ui + h] = jnp.exp(s - m).astype(BF16)
                    ms.append(m)
                store(m_scr, g, d, r, j, jnp.where(head0, ms[0], ms[1]))
            for ui, (r, j) in enumerate(chunk):
                ve = jnp.concatenate([window(vp_ref, vc_ref, r, j), ones], axis=1)
                oe = [jnp.dot(p_scr[pbase + 2 * ui + h], ve, preferred_element_type=F32) for h in range(2)]
                store(o_scr, g, d, r, j, jnp.where(head0, oe[0][:, :LANES], oe[1][:, :LANES]))
                store(l_scr, g, d, r, j, jnp.where(head0, oe[0][:, LANES:], oe[1][:, LANES:]))

    def merge(c, carry):
        rows = pl.ds(pl.multiple_of(c * blk, blk), blk)
        ms = [m_scr[g, rows] for g in range(3)]
        m_ref = jnp.maximum(jnp.maximum(ms[0], ms[1]), ms[2])
        num = jnp.zeros((blk, LANES), F32)
        den = jnp.zeros((blk, LANES), F32)
        for g in range(3):
            w = jnp.exp(ms[g] - m_ref)
            num = num + w * o_scr[g, rows]
            den = den + w * l_scr[g, rows]
        o_ref[0, rows] = (num / den).astype(BF16)
        return carry
    lax.fori_loop(0, ATT_BLOCK // blk, merge, 0)


def _attention(qkv):
    B = qkv[0].shape[0]
    S = qkv[0].shape[2]
    gw = B_GROUP_WIDTH
    blk = ATT_SPAN
    nblk = S // ATT_BLOCK
    in_specs, args = [], []
    for g, d in enumerate(DILATIONS):
        rows = ATT_BLOCK // d
        per = rows // blk
        cur = pl.BlockSpec((1, d, rows, LANES), lambda b, n, hp: (b, 0, n, hp))
        prev = pl.BlockSpec((1, d, blk, LANES),
                            lambda b, n, hp, per=per: (b, 0, jnp.maximum(n * per - 1, 0), hp))
        q, k, v = qkv[3 * g:3 * g + 3]
        in_specs += [cur, cur, prev, cur, prev]
        args += [q, k, k, v, v]
    return pl.pallas_call(
        _attention_body,
        grid=(B, nblk, gw // LANES),
        in_specs=in_specs,
        out_specs=pl.BlockSpec((1, ATT_BLOCK, LANES), lambda b, n, hp: (b, n, hp)),
        out_shape=jax.ShapeDtypeStruct((B, S, gw), BF16),
        scratch_shapes=[pltpu.VMEM((3, ATT_BLOCK, LANES), F32)] * 3 + [
            pltpu.VMEM((2, blk, 2 * blk), F32),
            pltpu.VMEM((4 * ATT_CHUNK, blk, 2 * blk), BF16)],
        compiler_params=_params(("arbitrary", "arbitrary", "arbitrary")),
        name="attention",
    )(*args)


def _post_mixer_body(alpha, x_ref, ya_ref, gb_ref, ob_ref, wbb_ref, wout_ref, gm_ref, lng_ref, lnb_ref,
                     sc_ref, sh_ref, wr_ref, br_ref,
                     x1_ref, h2_ref, pos_ref, wts_ref, cnt_ref):
    tt = x_ref.shape[1]
    yb = jnp.dot(ob_ref[0], wbb_ref[0], preferred_element_type=F32)
    merged = (ya_ref[0].astype(F32) + gb_ref[0].astype(F32) * yb).astype(BF16)
    y = jnp.dot(merged, wout_ref[0], preferred_element_type=F32)
    x1 = _layer_norm(alpha * x_ref[0] + (1.0 + gm_ref[0]) * y, lng_ref[0], lnb_ref[0])
    x1_ref[0] = x1
    h2 = x1 * (1.0 + sc_ref[0]) + sh_ref[0]
    h2_ref[...] = h2.astype(BF16)

    logits = lax.dot_general(wr_ref[0], h2, (((1,), (1,)), ((), ())), preferred_element_type=F32,
                             precision=lax.Precision.HIGHEST) + br_ref[0]
    eid = lax.broadcasted_iota(I32, (N_EXPERTS, tt), 0)
    vals, hots = [], []
    cur = logits
    for k in range(TOP_K):
        m = jnp.max(cur, axis=0, keepdims=True)
        sel = jnp.min(jnp.where(cur == m, eid, N_EXPERTS), axis=0, keepdims=True)
        hot = eid == sel
        vals.append(m)
        hots.append(hot)
        cur = jnp.where(hot, -jnp.inf, cur)
    ex = [jnp.exp(v - vals[0]) for v in vals]
    tot = ex[0] + ex[1] + ex[2] + ex[3]
    for k in range(TOP_K):
        wts_ref[k:k + 1, :] = ex[k] / tot

    member = jnp.zeros((N_EXPERTS, tt), F32)
    for hot in hots:
        member = member + hot.astype(F32)
    before = (lax.broadcasted_iota(I32, (tt, tt), 0) < lax.broadcasted_iota(I32, (tt, tt), 1))
    excl = jnp.dot(member.astype(BF16), before.astype(BF16), preferred_element_type=F32)
    cnt = jnp.broadcast_to(jnp.sum(member, axis=1, keepdims=True), (N_EXPERTS, LANES))
    cnt_ref[...] = cnt
    lower = (lax.broadcasted_iota(I32, (N_EXPERTS, N_EXPERTS), 1)
             < lax.broadcasted_iota(I32, (N_EXPERTS, N_EXPERTS), 0)).astype(F32)
    cnt_al = jnp.ceil(cnt * (1.0 / RUN_ALIGN)) * RUN_ALIGN
    off = jnp.dot(lower, cnt_al, preferred_element_type=F32, precision=lax.Precision.HIGHEST)[:, 0:1]
    base = off + excl
    for k in range(TOP_K):
        pos_ref[k:k + 1, :] = jnp.sum(jnp.where(hots[k], base, 0.0), axis=0, keepdims=True).astype(I32)


def _post_mixer(x, ya, gb, ob, w_bb, w_out, gm, ln_g, ln_b, sc, sh, w_rt, b_r, layer, depth):
    B, S, D = x.shape
    tt = TOKEN_TILE
    nt = S // tt
    T = B * S
    gw = B_GROUP_WIDTH
    tok = lambda b, i: (b, i, 0)
    per_b = lambda b, i: (b, 0, 0)
    per_l3 = lambda b, i: (layer, 0, 0)
    flat = lambda b, i: (0, b * nt + i)
    return pl.pallas_call(
        functools.partial(_post_mixer_body, (2.0 * depth) ** 0.25),
        grid=(B, nt),
        in_specs=[
            pl.BlockSpec((1, tt, D), tok),
            pl.BlockSpec((1, tt, D), tok),
            pl.BlockSpec((1, tt, D), tok),
            pl.BlockSpec((1, tt, gw), tok),
            pl.BlockSpec((1, gw, D), per_l3),
            pl.BlockSpec((1, D, D), per_l3),
            pl.BlockSpec((1, 1, D), per_b),
            pl.BlockSpec((1, 1, D), per_l3),
            pl.BlockSpec((1, 1, D), per_l3),
            pl.BlockSpec((1, 1, D), per_b),
            pl.BlockSpec((1, 1, D), per_b),
            pl.BlockSpec((1, N_EXPERTS, D), per_l3),
            pl.BlockSpec((1, N_EXPERTS, 1), per_l3),
        ],
        out_specs=[
            pl.BlockSpec((1, tt, D), tok),
            pl.BlockSpec((tt, D), lambda b, i: (b * nt + i, 0)),
            pl.BlockSpec((TOP_K, tt), flat),
            pl.BlockSpec((TOP_K, tt), flat),
            pl.BlockSpec((N_EXPERTS, LANES), lambda b, i: (b * nt + i, 0)),
        ],
        out_shape=[
            jax.ShapeDtypeStruct((B, S, D), F32),
            jax.ShapeDtypeStruct((T, D), BF16),
            jax.ShapeDtypeStruct((TOP_K, T), I32),
            jax.ShapeDtypeStruct((TOP_K, T), F32),
            jax.ShapeDtypeStruct((B * nt * N_EXPERTS, LANES), F32),
        ],
        compiler_params=_params(("arbitrary", "arbitrary")),
        name="post_mixer",
    )(x, ya, gb, ob, w_bb, w_out, gm, ln_g, ln_b, sc, sh, w_rt, b_r)


def _run_pieces(n, fn):
    for bit in reversed(range(RUN_ALIGN.bit_length() - 1, RUN_BITS)):
        size = 1 << bit
        done = pl.multiple_of((n >> (bit + 1)) << (bit + 1), RUN_ALIGN)

        @pl.when((n & size) != 0)
        def _():
            fn(done, size)


def _each_run(step, len_ref, a_off_ref, b_off_ref, make_copy, action):
    def run(e, carry):
        i = step * N_EXPERTS + e
        a0, b0 = a_off_ref[i], b_off_ref[i]
        _run_pieces(len_ref[i], lambda off, size: action(make_copy(
            pl.multiple_of(a0 + off, RUN_ALIGN), pl.multiple_of(b0 + off, RUN_ALIGN), size)))
        return carry
    lax.fori_loop(0, N_EXPERTS, run, 0)


def _dispatch_body(tile_off_ref, slot_off_ref, len_ref, ends_ref, padded_ref, used_ref, h_ref, pos_ref, xs_ref,
                   zbuf, rbuf, fill_sem, sem):
    t = pl.program_id(0)
    last = pl.num_programs(0) - 1
    tm = zbuf.shape[0]
    tt = h_ref.shape[0]

    @pl.when(t == 0)
    def _():
        zbuf[...] = jnp.zeros_like(zbuf)
        for e in range(N_EXPERTS):
            @pl.when(padded_ref[e] > 0)
            def _():
                start = pl.multiple_of(ends_ref[e] - tm, tm)
                cp = pltpu.make_async_copy(zbuf, xs_ref.at[pl.ds(start, tm)], fill_sem)
                cp.start()
                cp.wait()

        def fill_unused(i, carry):
            cp = pltpu.make_async_copy(zbuf, xs_ref.at[pl.ds(pl.multiple_of(i * tm, tm), tm)], fill_sem)
            cp.start()
            cp.wait()
            return carry
        lax.fori_loop(ends_ref[N_EXPERTS - 1] // tm, xs_ref.shape[0] // tm, fill_unused, 0)

    buf = t % 2
    h = h_ref[...]
    pos = [pos_ref[k:k + 1, :] for k in range(TOP_K)]
    for c in range(rbuf.shape[1] // PERM_ROWS):
        p_id = lax.broadcasted_iota(I32, (PERM_ROWS, tt), 0) + c * PERM_ROWS
        sel = (p_id == pos[0]) | (p_id == pos[1]) | (p_id == pos[2]) | (p_id == pos[3])
        rows = jnp.dot(jnp.where(sel, 1.0, 0.0).astype(BF16), h, preferred_element_type=F32)
        rbuf[buf, c * PERM_ROWS:(c + 1) * PERM_ROWS, :] = _pack_bf16_pairs(rows)

    def runs(s, action):
        def make_copy(tile_row, slot_row, size):
            return pltpu.make_async_copy(rbuf.at[s % 2, pl.ds(tile_row, size)], xs_ref.at[pl.ds(slot_row, size)],
                                         sem.at[s % 2])
        _each_run(s, len_ref, tile_off_ref, slot_off_ref, make_copy, action)

    def wait_step(s):
        pltpu.make_async_copy(rbuf.at[s % 2], xs_ref.at[pl.ds(0, rbuf.shape[1])], sem.at[s % 2]).wait()

    runs(t, lambda cp: cp.start())
    used = used_ref[t]
    spare = xs_ref.shape[0] - tm + buf * (tm // 2)
    _run_pieces(rbuf.shape[1] - used, lambda off, size: pltpu.make_async_copy(
        rbuf.at[buf, pl.ds(pl.multiple_of(used + off, RUN_ALIGN), size)],
        xs_ref.at[pl.ds(pl.multiple_of(spare + off, RUN_ALIGN), size)], sem.at[buf]).start())

    @pl.when(t > 0)
    def _():
        wait_step(t - 1)

    @pl.when(t == last)
    def _():
        wait_step(t)


def _dispatch(h2, pos, tile_off, slot_off, run_len, ends, padded, used, n_slots):
    T, D = h2.shape
    tt = TOKEN_TILE
    grid_spec = pltpu.PrefetchScalarGridSpec(
        num_scalar_prefetch=6,
        grid=(T // tt,),
        in_specs=[pl.BlockSpec((tt, D), lambda t, *_: (t, 0)),
                  pl.BlockSpec((TOP_K, tt), lambda t, *_: (0, t))],
        out_specs=pl.BlockSpec(memory_space=pl.ANY),
        scratch_shapes=[pltpu.VMEM((EXPERT_TILE, D // 2), U32), pltpu.VMEM((2, RUN_TILE_ROWS, D // 2), U32),
                        pltpu.SemaphoreType.DMA, pltpu.SemaphoreType.DMA((2,))],
    )
    return pl.pallas_call(
        _dispatch_body,
        grid_spec=grid_spec,
        out_shape=jax.ShapeDtypeStruct((n_slots, D // 2), U32),
        compiler_params=pltpu.CompilerParams(dimension_semantics=("arbitrary",), vmem_limit_bytes=VMEM_LIMIT,
                                             has_side_effects=True),
        name="dispatch",
    )(tile_off, slot_off, run_len, ends, padded, used, h2, pos)


def _experts_body(layer, te_ref, nact_ref, next_ref, xs_ref, wup_hbm, bup_ref, wdn_hbm, bdn_ref, ys_ref,
                  wup_f, wdn_f, wup_s, wdn_s, sem):
    i = pl.program_id(0)
    ff = wdn_s.shape[0]
    e = te_ref[i]

    def fetch(expert):
        row = layer * N_EXPERTS + expert
        return (pltpu.make_async_copy(wup_hbm.at[row], wup_f, sem.at[0]),
                pltpu.make_async_copy(wdn_hbm.at[row], wdn_f, sem.at[1]))

    @pl.when(i == 0)
    def _():
        for cp in fetch(e):
            cp.start()

    @pl.when((i == 0) | (e != te_ref[jnp.maximum(i - 1, 0)]))
    def _():
        for cp in fetch(e):
            cp.wait()
        n = 2 * LANES
        src = lax.broadcasted_iota(I32, (n, n), 0)
        dst = lax.broadcasted_iota(I32, (n, n), 1)
        perm = (src == jnp.where(dst < LANES, 2 * dst, 2 * (dst - LANES) + 1)).astype(BF16)
        for blk in range(wup_f.shape[1] // n):
            cols = slice(blk * n, (blk + 1) * n)
            wup_s[:, cols] = jnp.dot(wup_f[:, cols].astype(BF16), perm, preferred_element_type=F32).astype(BF16)
        wdn_s[...] = wdn_f[...].astype(BF16)

        @pl.when(next_ref[e] >= 0)
        def _():
            for cp in fetch(next_ref[e]):
                cp.start()

    @pl.when(i < nact_ref[0])
    def _():
        wd = xs_ref[...]
        x = jnp.concatenate([_unpack_hi(wd), _unpack_lo(wd)], axis=1).astype(BF16)
        hh = jnp.dot(x, wup_s[...], preferred_element_type=F32) + bup_ref[0]
        acts = []
        for blk in range(ff // LANES):
            glu = jnp.minimum(hh[:, 2 * blk * LANES:(2 * blk + 1) * LANES], SWIGLU_LIMIT)
            lin = jnp.clip(hh[:, (2 * blk + 1) * LANES:(2 * blk + 2) * LANES], -SWIGLU_LIMIT, SWIGLU_LIMIT)
            acts.append((glu * _sigmoid(SWIGLU_ALPHA * glu) * (lin + 1.0)).astype(BF16))
        act = jnp.concatenate(acts, axis=1)
        y = jnp.dot(act, wdn_s[...], preferred_element_type=F32) + bdn_ref[0]
        ys_ref[...] = _pack_bf16_pairs(y)

    @pl.when(i >= nact_ref[0])
    def _():
        ys_ref[...] = jnp.zeros_like(ys_ref)


def _experts(xs, tile_expert, n_active, next_expert, w_up, b_up, w_dn, b_dn, layer):
    P, W = xs.shape
    tm = EXPERT_TILE
    D = 2 * W
    ff2 = w_up.shape[2]
    ff = w_dn.shape[1]
    ex = lambda i, te, na, nx: (layer * N_EXPERTS + te[i], 0, 0)
    grid_spec = pltpu.PrefetchScalarGridSpec(
        num_scalar_prefetch=3,
        grid=(P // tm,),
        in_specs=[
            pl.BlockSpec((tm, W), lambda i, te, na, nx: (jnp.where(i < na[0], i, 0), 0)),
            pl.BlockSpec(memory_space=pl.ANY),
            pl.BlockSpec((1, 1, ff2), ex),
            pl.BlockSpec(memory_space=pl.ANY),
            pl.BlockSpec((1, 1, D), ex),
        ],
        out_specs=pl.BlockSpec((tm, W), lambda i, te, na, nx: (i, 0)),
        scratch_shapes=[pltpu.VMEM((D, ff2), F32), pltpu.VMEM((ff, D), F32),
                        pltpu.VMEM((D, ff2), BF16), pltpu.VMEM((ff, D), BF16), pltpu.SemaphoreType.DMA((2,))],
    )
    return pl.pallas_call(
        functools.partial(_experts_body, layer),
        grid_spec=grid_spec,
        out_shape=jax.ShapeDtypeStruct((P, W), U32),
        compiler_params=_params(("arbitrary",)),
        name="experts",
    )(tile_expert, n_active, next_expert, xs, w_up, b_up, w_dn, b_dn)


def _combine_body(alpha, tile_off_ref, slot_off_ref, len_ref, used_ref, x_ref, ys_ref, pos_ref, wts_ref, gm_ref,
                  lng_ref, lnb_ref, o_ref, ybuf, sem):
    tt = x_ref.shape[1]
    n = ybuf.shape[1]
    step = pl.program_id(0) * pl.num_programs(1) + pl.program_id(1)
    n_steps = pl.num_programs(0) * pl.num_programs(1)

    def runs(s, action):
        buf = s % 2

        def make_copy(tile_row, slot_row, size):
            return pltpu.make_async_copy(ys_ref.at[pl.ds(slot_row, size)], ybuf.at[buf, pl.ds(tile_row, size)],
                                         sem.at[buf])
        _each_run(s, len_ref, tile_off_ref, slot_off_ref, make_copy, action)

    def fetch(s):
        runs(s, lambda cp: cp.start())
        used = used_ref[s]
        _run_pieces(n - used, lambda off, size: pltpu.make_async_copy(
            ys_ref.at[pl.ds(off, size)], ybuf.at[s % 2, pl.ds(pl.multiple_of(used + off, RUN_ALIGN), size)],
            sem.at[s % 2]).start())

    @pl.when(step == 0)
    def _():
        fetch(step)

    @pl.when(step + 1 < n_steps)
    def _():
        fetch(step + 1)

    pltpu.make_async_copy(ys_ref.at[pl.ds(0, n)], ybuf.at[step % 2], sem.at[step % 2]).wait()

    p_id = lax.broadcasted_iota(I32, (tt, n), 1)
    a = jnp.zeros((tt, n), F32)
    for k in reversed(range(TOP_K)):
        a = jnp.where(p_id == pos_ref[:, k:k + 1], wts_ref[:, k:k + 1], a)
    a = a.astype(BF16)
    wd = ybuf[step % 2]
    hi = jnp.dot(a, _unpack_hi(wd).astype(BF16), preferred_element_type=F32)
    lo = jnp.dot(a, _unpack_lo(wd).astype(BF16), preferred_element_type=F32)
    y = jnp.concatenate([hi, lo], axis=1)
    o_ref[0] = _layer_norm(alpha * x_ref[0] + (1.0 + gm_ref[0]) * y, lng_ref[0], lnb_ref[0])


def _combine(x1, ys, tile_off, slot_off, run_len, used, pos_t, wts_t, gm, ln_g, ln_b, layer, depth):
    B, S, D = x1.shape
    tt = TOKEN_TILE
    nt = S // tt
    grid_spec = pltpu.PrefetchScalarGridSpec(
        num_scalar_prefetch=4,
        grid=(B, nt),
        in_specs=[
            pl.BlockSpec((1, tt, D), lambda b, i, *_: (b, i, 0)),
            pl.BlockSpec(memory_space=pl.ANY),
            pl.BlockSpec((tt, TOP_K), lambda b, i, *_: (b * nt + i, 0)),
            pl.BlockSpec((tt, TOP_K), lambda b, i, *_: (b * nt + i, 0)),
            pl.BlockSpec((1, 1, D), lambda b, i, *_: (b, 0, 0)),
            pl.BlockSpec((1, 1, D), lambda b, i, *_: (layer, 0, 0)),
            pl.BlockSpec((1, 1, D), lambda b, i, *_: (layer, 0, 0)),
        ],
        out_specs=pl.BlockSpec((1, tt, D), lambda b, i, *_: (b, i, 0)),
        scratch_shapes=[pltpu.VMEM((2, RUN_TILE_ROWS, D // 2), U32), pltpu.SemaphoreType.DMA((2,))],
    )
    return pl.pallas_call(
        functools.partial(_combine_body, (2.0 * depth) ** 0.25),
        grid_spec=grid_spec,
        out_shape=jax.ShapeDtypeStruct((B, S, D), F32),
        compiler_params=_params(("arbitrary", "arbitrary")),
        name="combine",
    )(tile_off, slot_off, run_len, used, x1, ys, pos_t, wts_t, gm, ln_g, ln_b)


def _rope_tables(S):
    half = ROPE_DIM // 2
    inv = ROPE_THETA ** (-jnp.arange(half, dtype=F32) / half)
    ang = jnp.arange(S, dtype=F32)[:, None] * inv[None, :]
    cos, sin = jnp.cos(ang), jnp.sin(ang)
    lane = jnp.arange(LANES) % HEAD_DIM
    pick = lane % half
    cos_l, sin_l = cos[:, pick], sin[:, pick]
    cos_t = jnp.where(lane < ROPE_DIM, cos_l, 1.0)
    s_up = jnp.where((lane >= half) & (lane < ROPE_DIM), sin_l, 0.0)
    s_dn = jnp.where(lane < half, -sin_l, 0.0)
    return cos_t.astype(F32), s_up.astype(F32), s_dn.astype(F32)


def kernel(x, c, w_ada, b_ada, w_in, spatial_w, spatial_b, w_branch_a, w_branch_b, w_out, ln1_g, ln1_b,
           w_router, b_router, w_up, b_up, w_down, b_down, ln2_g, ln2_b):
    B, S, D = x.shape
    L = w_ada.shape[0]
    T = B * S
    gw = B_GROUP_WIDTH
    nq = len(DILATIONS) * gw
    assert S % ATT_BLOCK == 0 and S % TOKEN_TILE == 0 and D % (2 * LANES) == 0

    q0, k0, v0, g0 = 2 * D, 2 * D + nq, 2 * D + 2 * nq, 2 * D + 3 * nq
    w_a = jnp.concatenate([w_in[:, :, :2 * D], w_in[:, :, g0:g0 + D]], axis=2).astype(BF16)
    cols = []
    for g in range(len(DILATIONS)):
        for base in (q0, k0, v0):
            cols.append(w_in[:, :, base + g * gw:base + (g + 1) * gw])
    cols.append(w_in[:, :, g0 + D:g0 + 2 * D])
    w_c = jnp.concatenate(cols, axis=2).astype(BF16)
    w_ba = w_branch_a.astype(BF16)
    w_bb = w_branch_b.astype(BF16)
    w_o = w_out.astype(BF16)
    sp_b = spatial_b[..., None]
    w_rt = jnp.swapaxes(w_router, 1, 2)
    b_r = b_router[..., None]
    ff = w_down.shape[2]
    w_up_p = w_up.reshape(L * N_EXPERTS, D, 2 * ff)
    b_up_p = jnp.swapaxes(b_up.reshape(L * N_EXPERTS, 1, ff // LANES, LANES, 2), 3, 4).reshape(
        L * N_EXPERTS, 1, 2 * ff)
    w_dn_p = w_down.reshape(L * N_EXPERTS, ff, D)
    b_dn_p = b_down.reshape(L * N_EXPERTS, 1, D)
    ln1g, ln1b = ln1_g[:, None, :], ln1_b[:, None, :]
    ln2g, ln2b = ln2_g[:, None, :], ln2_b[:, None, :]
    cos_t, sup_t, sdn_t = _rope_tables(S)

    mod = _ada_mod(c, w_ada, b_ada)
    n_slots = T * TOP_K + (T // TOKEN_TILE) * N_EXPERTS * RUN_ALIGN + N_EXPERTS * EXPERT_TILE
    n_slots += EXPERT_TILE
    n_tiles = n_slots // EXPERT_TILE

    for l in range(L):
        sh1, sc1, g1, sh2, sc2, g2 = [mod[l, :, None, i * D:(i + 1) * D] for i in range(6)]
        ya = _branch_a(x, sc1, sh1, w_a, spatial_w, sp_b, w_ba, l)
        *qkv, gb = _qkv(x, sc1, sh1, w_c, cos_t, sup_t, sdn_t, l)
        ob = _attention(qkv)
        x1, h2, pos, wts, cnt = _post_mixer(x, ya, gb, ob, w_bb, w_o, g1, ln1g, ln1b, sc2, sh2, w_rt, b_r, l, L)
        n_tok_tiles = T // TOKEN_TILE
        tc = cnt.reshape(n_tok_tiles, N_EXPERTS, LANES)[:, :, 0].astype(I32)
        tc = ((tc + RUN_ALIGN - 1) // RUN_ALIGN) * RUN_ALIGN
        off_in_tile = jnp.cumsum(tc, axis=1) - tc
        seen_before = jnp.cumsum(tc, axis=0) - tc
        counts = jnp.sum(tc, axis=0)
        padded = ((counts + EXPERT_TILE - 1) // EXPERT_TILE) * EXPERT_TILE
        ends = jnp.cumsum(padded).astype(I32)
        starts = ends - padded
        tile_off = off_in_tile.astype(I32).reshape(-1)
        slot_off = (starts[None, :] + seen_before).astype(I32).reshape(-1)
        run_len = tc.reshape(-1)
        n_active = (ends[-1] // EXPERT_TILE).astype(I32)
        tile_start = jnp.arange(n_tiles, dtype=I32) * EXPERT_TILE
        te = jnp.sum((ends[None, :] <= tile_start[:, None]).astype(I32), axis=1)
        te_last = jnp.sum((ends <= (n_active - 1) * EXPERT_TILE).astype(I32))
        te = jnp.minimum(jnp.where(tile_start < ends[-1], te, te_last), N_EXPERTS - 1).astype(I32)

        padded = padded.astype(I32)
        used = jnp.sum(tc, axis=1).astype(I32)
        xs = _dispatch(h2, pos, tile_off, slot_off, run_len, ends, padded, used, n_slots)
        eid = jnp.arange(N_EXPERTS, dtype=I32)
        later = (eid[None, :] > eid[:, None]) & (padded[None, :] > 0)
        next_expert = jnp.min(jnp.where(later, eid[None, :], N_EXPERTS), axis=1)
        next_expert = jnp.where(next_expert < N_EXPERTS, next_expert, -1).astype(I32)
        ys = _experts(xs, te, n_active.reshape(1), next_expert, w_up_p, b_up_p, w_dn_p, b_dn_p, l)
        x = _combine(x1, ys, tile_off, slot_off, run_len, used, pos.T, wts.T, g2, ln2g, ln2b, l, L)
    return x
```

```python
import functools
import math

import jax
import jax.numpy as jnp
from jax import lax
from jax.experimental import pallas as pl
from jax.experimental.pallas import tpu as pltpu

BF16 = jnp.bfloat16
F32 = jnp.float32
U32 = jnp.uint32
I32 = jnp.int32

A_CHUNK = 128
A_GROUPS = 4
HEAD_DIM = 64
B_HEADS = 8
DILATIONS = (1, 4, 16)
ATT_SPAN = 128
B_GROUP_WIDTH = B_HEADS * HEAD_DIM
ROPE_DIM = HEAD_DIM // 4
ROPE_THETA = 500000.0
N_EXPERTS = 32
TOP_K = 4
SWIGLU_LIMIT = 7.0
SWIGLU_ALPHA = 1.702
LN_EPS = 1e-5

LANES = 128
VMEM_LIMIT = 56 * 1024 * 1024
TOKEN_TILE = 512
MIXER_TILE = 1024
ATT_BLOCK = 128 * DILATIONS[-1]
EXPERT_TILE = 512
ATT_CHUNK = 4
RUN_BITS = TOKEN_TILE.bit_length()
RUN_ALIGN = 8
RUN_TILE_ROWS = TOP_K * TOKEN_TILE + N_EXPERTS * RUN_ALIGN
PERM_ROWS = 256


def _gelu(x):
    return 0.5 * x * (1.0 + jnp.tanh(math.sqrt(2.0 / math.pi) * (x + 0.044715 * (x * x * x))))


def _sigmoid(x):
    return 1.0 / (1.0 + jnp.exp(-x))


def _layer_norm(r, g, b):
    mu = jnp.mean(r, axis=-1, keepdims=True)
    c = r - mu
    var = jnp.mean(c * c, axis=-1, keepdims=True)
    return c * lax.rsqrt(var + LN_EPS) * g + b


def _pack_bf16_pairs(x):
    w = x.shape[1] // 2
    bits = lax.bitcast_convert_type(x.astype(BF16).astype(F32), U32)
    return bits[:, :w] | lax.shift_right_logical(bits[:, w:], jnp.uint32(16))


def _unpack_hi(wd):
    return lax.bitcast_convert_type(wd & jnp.uint32(0xFFFF0000), F32)


def _unpack_lo(wd):
    return lax.bitcast_convert_type(lax.shift_left(wd, jnp.uint32(16)), F32)


def _params(sem):
    return pltpu.CompilerParams(dimension_semantics=sem, vmem_limit_bytes=VMEM_LIMIT)


def _ada_body(c_ref, w_ref, b_ref, o_ref):
    o_ref[0] = jnp.dot(c_ref[...], w_ref[0], preferred_element_type=F32,
                       precision=lax.Precision.HIGHEST) + b_ref[0]


def _ada_mod(c, w_ada, b_ada):
    L, D, N = w_ada.shape
    B = c.shape[0]
    tn = D
    return pl.pallas_call(
        _ada_body,
        grid=(L, N // tn),
        in_specs=[
            pl.BlockSpec((B, D), lambda l, j: (0, 0)),
            pl.BlockSpec((1, D, tn), lambda l, j: (l, 0, j)),
            pl.BlockSpec((1, 1, tn), lambda l, j: (l, 0, j)),
        ],
        out_specs=pl.BlockSpec((1, B, tn), lambda l, j: (l, 0, j)),
        out_shape=jax.ShapeDtypeStruct((L, B, N), F32),
        compiler_params=_params(("arbitrary", "arbitrary")),
        name="ada_mod",
    )(c, w_ada, b_ada.reshape(L, 1, N))


def _branch_a_body(x_ref, sc_ref, sh_ref, wa_ref, spw_ref, spb_ref, wba_ref, o_ref, a_scr):
    tm, D = x_ref.shape[1], x_ref.shape[2]
    gd = D // A_GROUPS
    h = (x_ref[0] * (1.0 + sc_ref[0]) + sh_ref[0]).astype(BF16)
    row = lax.broadcasted_iota(I32, (A_CHUNK, A_CHUNK), 0)
    col = lax.broadcasted_iota(I32, (A_CHUNK, A_CHUNK), 1)
    causal = row >= col
    for g in range(A_GROUPS):
        v = _gelu(jnp.dot(h, wa_ref[0, :, D + g * gd:D + (g + 1) * gd], preferred_element_type=F32))
        mu = jnp.mean(v, axis=-1, keepdims=True)
        vc = v - mu
        var = jnp.mean(vc * vc, axis=-1, keepdims=True)
        vn = (vc * lax.rsqrt(var + LN_EPS)).astype(BF16)
        u = _gelu(jnp.dot(h, wa_ref[0, :, g * gd:(g + 1) * gd], preferred_element_type=F32))
        ws = jnp.where(causal, spw_ref[0, g], 0.0).astype(BF16)
        bias = spb_ref[0, g]
        for c in range(tm // A_CHUNK):
            rows = slice(c * A_CHUNK, (c + 1) * A_CHUNK)
            sv = jnp.dot(ws, vn[rows], preferred_element_type=F32) + bias
            a_scr[rows, g * gd:(g + 1) * gd] = (u[rows] * sv).astype(BF16)
    ya = jnp.dot(a_scr[...], wba_ref[0], preferred_element_type=F32)
    gate = _sigmoid(jnp.dot(h, wa_ref[0, :, 2 * D:3 * D], preferred_element_type=F32))
    o_ref[0] = (gate * ya).astype(BF16)


def _branch_a(x, sc, sh, w_a, sp_w, sp_b, w_ba, layer):
    B, S, D = x.shape
    tm = MIXER_TILE
    return pl.pallas_call(
        _branch_a_body,
        grid=(B, S // tm),
        in_specs=[
            pl.BlockSpec((1, tm, D), lambda b, i: (b, i, 0)),
            pl.BlockSpec((1, 1, D), lambda b, i: (b, 0, 0)),
            pl.BlockSpec((1, 1, D), lambda b, i: (b, 0, 0)),
            pl.BlockSpec((1, D, 3 * D), lambda b, i: (layer, 0, 0)),
            pl.BlockSpec((1, A_GROUPS, A_CHUNK, A_CHUNK), lambda b, i: (layer, 0, 0, 0)),
            pl.BlockSpec((1, A_GROUPS, A_CHUNK, 1), lambda b, i: (layer, 0, 0, 0)),
            pl.BlockSpec((1, D, D), lambda b, i: (layer, 0, 0)),
        ],
        out_specs=pl.BlockSpec((1, tm, D), lambda b, i: (b, i, 0)),
        out_shape=jax.ShapeDtypeStruct((B, S, D), BF16),
        scratch_shapes=[pltpu.VMEM((tm, D), BF16)],
        compiler_params=_params(("arbitrary", "arbitrary")),
        name="branch_a",
    )(x, sc, sh, w_a, sp_w, sp_b, w_ba)


def _rope(z, cos, s_up, s_dn):
    half = ROPE_DIM // 2
    parts = []
    for c in range(z.shape[1] // LANES):
        zc = z[:, c * LANES:(c + 1) * LANES]
        parts.append(zc * cos + pltpu.roll(zc, half, 1) * s_up + pltpu.roll(zc, LANES - half, 1) * s_dn)
    return jnp.concatenate(parts, axis=1)


def _qkv_body(x_ref, sc_ref, sh_ref, w_ref, cos_ref, sup_ref, sdn_ref, *rest):
    outs = rest[:9]
    gate_ref = rest[9]
    scr = rest[10]
    tm, D = x_ref.shape[1], x_ref.shape[2]
    gw = B_GROUP_WIDTH
    h = (x_ref[0] * (1.0 + sc_ref[0]) + sh_ref[0]).astype(BF16)
    cos, s_up, s_dn = cos_ref[...], sup_ref[...], sdn_ref[...]
    for g, d in enumerate(DILATIONS):
        for part in range(3):
            col = (3 * g + part) * gw
            z = jnp.dot(h, w_ref[0, :, col:col + gw], preferred_element_type=F32)
            if part < 2:
                z = _rope(z, cos, s_up, s_dn)
            if part == 0:
                z = z * (HEAD_DIM ** -0.5)
            o_ref = outs[3 * g + part]
            if d == 1:
                o_ref[0, 0] = z.astype(BF16)
            else:
                nc = gw // LANES
                for c in range(nc):
                    scr[c] = z[:, c * LANES:(c + 1) * LANES]
                for r in range(d):
                    o_ref[0, r] = jnp.concatenate(
                        [scr[c, pl.ds(r, tm // d, stride=d), :] for c in range(nc)], axis=1).astype(BF16)
    gate = _sigmoid(jnp.dot(h, w_ref[0, :, 9 * gw:9 * gw + D], preferred_element_type=F32))
    gate_ref[0] = gate.astype(BF16)


def _qkv(x, sc, sh, w_c, cos_t, sup_t, sdn_t, layer):
    B, S, D = x.shape
    tm = MIXER_TILE
    gw = B_GROUP_WIDTH
    nw = w_c.shape[2]
    out_shapes, out_specs = [], []
    for d in DILATIONS:
        for _ in range(3):
            out_shapes.append(jax.ShapeDtypeStruct((B, d, S // d, gw), BF16))
            out_specs.append(pl.BlockSpec((1, d, tm // d, gw), lambda b, i: (b, 0, i, 0)))
    out_shapes.append(jax.ShapeDtypeStruct((B, S, D), BF16))
    out_specs.append(pl.BlockSpec((1, tm, D), lambda b, i: (b, i, 0)))
    return pl.pallas_call(
        _qkv_body,
        grid=(B, S // tm),
        in_specs=[
            pl.BlockSpec((1, tm, D), lambda b, i: (b, i, 0)),
            pl.BlockSpec((1, 1, D), lambda b, i: (b, 0, 0)),
            pl.BlockSpec((1, 1, D), lambda b, i: (b, 0, 0)),
            pl.BlockSpec((1, D, nw), lambda b, i: (layer, 0, 0), pipeline_mode=pl.Buffered(1)),
            pl.BlockSpec((tm, LANES), lambda b, i: (i, 0)),
            pl.BlockSpec((tm, LANES), lambda b, i: (i, 0)),
            pl.BlockSpec((tm, LANES), lambda b, i: (i, 0)),
        ],
        out_specs=out_specs,
        out_shape=out_shapes,
        scratch_shapes=[pltpu.VMEM((gw // LANES, tm, LANES), F32)],
        compiler_params=_params(("arbitrary", "arbitrary")),
        name="qkv",
    )(x, sc, sh, w_c, cos_t, sup_t, sdn_t)


def _attention_body(*refs):
    ins = refs[:15]
    o_ref = refs[15]
    o_scr, m_scr, l_scr, bias_scr, p_scr = refs[16:21]
    n = pl.program_id(1)
    blk = ATT_SPAN
    qi = lax.broadcasted_iota(I32, (blk, 2 * blk), 0)
    kj = lax.broadcasted_iota(I32, (blk, 2 * blk), 1)
    band = (kj >= qi) & (kj <= qi + ATT_SPAN)
    neg = jnp.float32(-jnp.inf)
    bias_scr[0] = jnp.where(band, 0.0, neg)
    bias_scr[1] = jnp.where(band & ((kj >= blk) | (n > 0)), 0.0, neg)
    lane = lax.broadcasted_iota(I32, (blk, LANES), 1)
    head0 = lane < HEAD_DIM
    ones = jnp.ones((2 * blk, LANES), BF16)

    def store(scr, g, d, r, j, val):
        view = scr.at[g, pl.ds(j * blk * d, blk * d)]
        if d == 1:
            view[...] = val
        else:
            view[pl.ds(r, blk, stride=d), :] = val

    def window(prev_ref, cur_ref, r, j):
        if j == 0:
            return jnp.concatenate([prev_ref[0, r], cur_ref[0, r, 0:blk]], axis=0)
        return cur_ref[0, r, (j - 1) * blk:(j + 1) * blk]

    chunk_id = 0
    for g, d in enumerate(DILATIONS):
        q_ref, kc_ref, kp_ref, vc_ref, vp_ref = ins[5 * g:5 * g + 5]
        nj = ATT_BLOCK // d // blk
        units = [(r, j) for r in range(d) for j in range(nj)]
        for c0 in range(0, len(units), ATT_CHUNK):
            chunk = units[c0:c0 + ATT_CHUNK]
            pbase = (chunk_id % 2) * 2 * ATT_CHUNK
            chunk_id += 1
            for ui, (r, j) in enumerate(chunk):
                q = q_ref[0, r, j * blk:(j + 1) * blk]
                kk = window(kp_ref, kc_ref, r, j)
                bias = bias_scr[1 if j == 0 else 0]
                zero = jnp.zeros_like(q)
                ms = []
                for h, qh in enumerate((jnp.where(head0, q, zero), jnp.where(head0, zero, q))):
                    s = lax.dot_general(qh, kk, (((1,), (1,)), ((), ())), preferred_element_type=F32) + bias
                    m = jnp.max(s, axis=1, keepdims=True)
                    p_scr[pbase + 2 * ui + h] = jnp.exp(s - m).astype(BF16)
                    ms.append(m)
                store(m_scr, g, d, r, j, jnp.where(head0, ms[0], ms[1]))
            for ui, (r, j) in enumerate(chunk):
                ve = jnp.concatenate([window(vp_ref, vc_ref, r, j), ones], axis=1)
                oe = [jnp.dot(p_scr[pbase + 2 * ui + h], ve, preferred_element_type=F32) for h in range(2)]
                store(o_scr, g, d, r, j, jnp.where(head0, oe[0][:, :LANES], oe[1][:, :LANES]))
                store(l_scr, g, d, r, j, jnp.where(head0, oe[0][:, LANES:], oe[1][:, LANES:]))

    def merge(c, carry):
        rows = pl.ds(pl.multiple_of(c * blk, blk), blk)
        ms = [m_scr[g, rows] for g in range(3)]
        m_ref = jnp.maximum(jnp.maximum(ms[0], ms[1]), ms[2])
        num = jnp.zeros((blk, LANES), F32)
        den = jnp.zeros((blk, LANES), F32)
        for g in range(3):
            w = jnp.exp(ms[g] - m_ref)
            num = num + w * o_scr[g, rows]
            den = den + w * l_scr[g, rows]
        o_ref[0, rows] = (num / den).astype(BF16)
        return carry
    lax.fori_loop(0, ATT_BLOCK // blk, merge, 0)


def _attention(qkv):
    B = qkv[0].shape[0]
    S = qkv[0].shape[2]
    gw = B_GROUP_WIDTH
    blk = ATT_SPAN
    nblk = S // ATT_BLOCK
    in_specs, args = [], []
    for g, d in enumerate(DILATIONS):
        rows = ATT_BLOCK // d
        per = rows // blk
        cur = pl.BlockSpec((1, d, rows, LANES), lambda b, n, hp: (b, 0, n, hp))
        prev = pl.BlockSpec((1, d, blk, LANES),
                            lambda b, n, hp, per=per: (b, 0, jnp.maximum(n * per - 1, 0), hp))
        q, k, v = qkv[3 * g:3 * g + 3]
        in_specs += [cur, cur, prev, cur, prev]
        args += [q, k, k, v, v]
    return pl.pallas_call(
        _attention_body,
        grid=(B, nblk, gw // LANES),
        in_specs=in_specs,
        out_specs=pl.BlockSpec((1, ATT_BLOCK, LANES), lambda b, n, hp: (b, n, hp)),
        out_shape=jax.ShapeDtypeStruct((B, S, gw), BF16),
        scratch_shapes=[pltpu.VMEM((3, ATT_BLOCK, LANES), F32)] * 3 + [
            pltpu.VMEM((2, blk, 2 * blk), F32),
            pltpu.VMEM((4 * ATT_CHUNK, blk, 2 * blk), BF16)],
        compiler_params=_params(("arbitrary", "arbitrary", "arbitrary")),
        name="attention",
    )(*args)


def _post_mixer_body(alpha, x_ref, ya_ref, gb_ref, ob_ref, wbb_ref, wout_ref, gm_ref, lng_ref, lnb_ref,
                     sc_ref, sh_ref, wr_ref, br_ref,
                     x1_ref, h2_ref, pos_ref, wts_ref, cnt_ref):
    tt = x_ref.shape[1]
    yb = jnp.dot(ob_ref[0], wbb_ref[0], preferred_element_type=F32)
    merged = ya_ref[0] + gb_ref[0] * yb.astype(BF16)
    y = jnp.dot(merged, wout_ref[0], preferred_element_type=F32)
    x1 = _layer_norm(alpha * x_ref[0] + (1.0 + gm_ref[0]) * y, lng_ref[0], lnb_ref[0])
    x1_ref[0] = x1
    h2 = x1 * (1.0 + sc_ref[0]) + sh_ref[0]
    h_hi = h2.astype(BF16)
    h2_ref[...] = h_hi

    h_lo = (h2 - h_hi.astype(F32)).astype(BF16)
    w_hi = wr_ref[0].astype(BF16)
    w_lo = (wr_ref[0] - w_hi.astype(F32)).astype(BF16)
    nt_dims = (((1,), (1,)), ((), ()))
    logits = (lax.dot_general(w_hi, h_hi, nt_dims, preferred_element_type=F32)
              + lax.dot_general(w_lo, h_hi, nt_dims, preferred_element_type=F32)
              + lax.dot_general(w_hi, h_lo, nt_dims, preferred_element_type=F32)) + br_ref[0]
    eid = lax.broadcasted_iota(I32, (N_EXPERTS, tt), 0)
    vals, hots = [], []
    cur = logits
    for k in range(TOP_K):
        m = jnp.max(cur, axis=0, keepdims=True)
        sel = jnp.min(jnp.where(cur == m, eid, N_EXPERTS), axis=0, keepdims=True)
        hot = eid == sel
        vals.append(m)
        hots.append(hot)
        cur = jnp.where(hot, -jnp.inf, cur)
    ex = [jnp.exp(v - vals[0]) for v in vals]
    tot = ex[0] + ex[1] + ex[2] + ex[3]
    for k in range(TOP_K):
        wts_ref[k:k + 1, :] = ex[k] / tot

    member = jnp.zeros((N_EXPERTS, tt), F32)
    for hot in hots:
        member = member + hot.astype(F32)
    before = (lax.broadcasted_iota(I32, (tt, tt), 0) < lax.broadcasted_iota(I32, (tt, tt), 1))
    excl = jnp.dot(member.astype(BF16), before.astype(BF16), preferred_element_type=F32)
    cnt = jnp.broadcast_to(jnp.sum(member, axis=1, keepdims=True), (N_EXPERTS, LANES))
    cnt_ref[...] = cnt
    lower = (lax.broadcasted_iota(I32, (N_EXPERTS, N_EXPERTS), 1)
             < lax.broadcasted_iota(I32, (N_EXPERTS, N_EXPERTS), 0)).astype(F32)
    cnt_al = jnp.ceil(cnt * (1.0 / RUN_ALIGN)) * RUN_ALIGN
    off = jnp.dot(lower, cnt_al, preferred_element_type=F32, precision=lax.Precision.HIGHEST)[:, 0:1]
    base = off + excl
    for k in range(TOP_K):
        pos_ref[k:k + 1, :] = jnp.sum(jnp.where(hots[k], base, 0.0), axis=0, keepdims=True).astype(I32)


def _post_mixer(x, ya, gb, ob, w_bb, w_out, gm, ln_g, ln_b, sc, sh, w_rt, b_r, layer, depth):
    B, S, D = x.shape
    tt = TOKEN_TILE
    nt = S // tt
    T = B * S
    gw = B_GROUP_WIDTH
    tok = lambda b, i: (b, i, 0)
    per_b = lambda b, i: (b, 0, 0)
    per_l3 = lambda b, i: (layer, 0, 0)
    flat = lambda b, i: (0, b * nt + i)
    return pl.pallas_call(
        functools.partial(_post_mixer_body, (2.0 * depth) ** 0.25),
        grid=(B, nt),
        in_specs=[
            pl.BlockSpec((1, tt, D), tok),
            pl.BlockSpec((1, tt, D), tok),
            pl.BlockSpec((1, tt, D), tok),
            pl.BlockSpec((1, tt, gw), tok),
            pl.BlockSpec((1, gw, D), per_l3),
            pl.BlockSpec((1, D, D), per_l3),
            pl.BlockSpec((1, 1, D), per_b),
            pl.BlockSpec((1, 1, D), per_l3),
            pl.BlockSpec((1, 1, D), per_l3),
            pl.BlockSpec((1, 1, D), per_b),
            pl.BlockSpec((1, 1, D), per_b),
            pl.BlockSpec((1, N_EXPERTS, D), per_l3),
            pl.BlockSpec((1, N_EXPERTS, 1), per_l3),
        ],
        out_specs=[
            pl.BlockSpec((1, tt, D), tok),
            pl.BlockSpec((tt, D), lambda b, i: (b * nt + i, 0)),
            pl.BlockSpec((TOP_K, tt), flat),
            pl.BlockSpec((TOP_K, tt), flat),
            pl.BlockSpec((N_EXPERTS, LANES), lambda b, i: (b * nt + i, 0)),
        ],
        out_shape=[
            jax.ShapeDtypeStruct((B, S, D), F32),
            jax.ShapeDtypeStruct((T, D), BF16),
            jax.ShapeDtypeStruct((TOP_K, T), I32),
            jax.ShapeDtypeStruct((TOP_K, T), F32),
            jax.ShapeDtypeStruct((B * nt * N_EXPERTS, LANES), F32),
        ],
        compiler_params=_params(("arbitrary", "arbitrary")),
        name="post_mixer",
    )(x, ya, gb, ob, w_bb, w_out, gm, ln_g, ln_b, sc, sh, w_rt, b_r)


def _run_pieces(n, fn):
    for bit in reversed(range(RUN_ALIGN.bit_length() - 1, RUN_BITS)):
        size = 1 << bit
        done = pl.multiple_of((n >> (bit + 1)) << (bit + 1), RUN_ALIGN)

        @pl.when((n & size) != 0)
        def _():
            fn(done, size)


def _each_run(step, len_ref, a_off_ref, b_off_ref, make_copy, action):
    def run(e, carry):
        i = step * N_EXPERTS + e
        a0, b0 = a_off_ref[i], b_off_ref[i]
        _run_pieces(len_ref[i], lambda off, size: action(make_copy(
            pl.multiple_of(a0 + off, RUN_ALIGN), pl.multiple_of(b0 + off, RUN_ALIGN), size)))
        return carry
    lax.fori_loop(0, N_EXPERTS, run, 0)


def _dispatch_body(tile_off_ref, slot_off_ref, len_ref, ends_ref, padded_ref, used_ref, h_ref, pos_ref, xs_ref,
                   zbuf, rbuf, fill_sem, sem):
    t = pl.program_id(0)
    last = pl.num_programs(0) - 1
    tm = zbuf.shape[0]
    tt = h_ref.shape[0]

    @pl.when(t == 0)
    def _():
        zbuf[...] = jnp.zeros_like(zbuf)
        for e in range(N_EXPERTS):
            @pl.when(padded_ref[e] > 0)
            def _():
                start = pl.multiple_of(ends_ref[e] - tm, tm)
                cp = pltpu.make_async_copy(zbuf, xs_ref.at[pl.ds(start, tm)], fill_sem)
                cp.start()
                cp.wait()

        def fill_unused(i, carry):
            cp = pltpu.make_async_copy(zbuf, xs_ref.at[pl.ds(pl.multiple_of(i * tm, tm), tm)], fill_sem)
            cp.start()
            cp.wait()
            return carry
        lax.fori_loop(ends_ref[N_EXPERTS - 1] // tm, xs_ref.shape[0] // tm, fill_unused, 0)

    buf = t % 2
    h = h_ref[...]
    pos = [pos_ref[k:k + 1, :] for k in range(TOP_K)]
    for c in range(rbuf.shape[1] // PERM_ROWS):
        p_id = lax.broadcasted_iota(I32, (PERM_ROWS, tt), 0) + c * PERM_ROWS
        onehot = jnp.zeros((PERM_ROWS, tt), F32)
        for k in range(TOP_K):
            onehot = jnp.where(p_id == pos[k], 1.0, onehot)
        rows = jnp.dot(onehot.astype(BF16), h, preferred_element_type=F32)
        bits = lax.bitcast_convert_type(rows, U32)
        half = bits.shape[1] // 2
        rbuf[buf, c * PERM_ROWS:(c + 1) * PERM_ROWS, :] = (
            bits[:, :half] | lax.shift_right_logical(bits[:, half:], jnp.uint32(16)))

    def runs(s, action):
        def make_copy(tile_row, slot_row, size):
            return pltpu.make_async_copy(rbuf.at[s % 2, pl.ds(tile_row, size)], xs_ref.at[pl.ds(slot_row, size)],
                                         sem.at[s % 2])
        _each_run(s, len_ref, tile_off_ref, slot_off_ref, make_copy, action)

    def wait_step(s):
        pltpu.make_async_copy(rbuf.at[s % 2], xs_ref.at[pl.ds(0, rbuf.shape[1])], sem.at[s % 2]).wait()

    runs(t, lambda cp: cp.start())
    used = used_ref[t]
    spare = xs_ref.shape[0] - tm + buf * (tm // 2)
    _run_pieces(rbuf.shape[1] - used, lambda off, size: pltpu.make_async_copy(
        rbuf.at[buf, pl.ds(pl.multiple_of(used + off, RUN_ALIGN), size)],
        xs_ref.at[pl.ds(pl.multiple_of(spare + off, RUN_ALIGN), size)], sem.at[buf]).start())

    @pl.when(t > 0)
    def _():
        wait_step(t - 1)

    @pl.when(t == last)
    def _():
        wait_step(t)


def _dispatch(h2, pos, tile_off, slot_off, run_len, ends, padded, used, n_slots):
    T, D = h2.shape
    tt = TOKEN_TILE
    grid_spec = pltpu.PrefetchScalarGridSpec(
        num_scalar_prefetch=6,
        grid=(T // tt,),
        in_specs=[pl.BlockSpec((tt, D), lambda t, *_: (t, 0)),
                  pl.BlockSpec((TOP_K, tt), lambda t, *_: (0, t))],
        out_specs=pl.BlockSpec(memory_space=pl.ANY),
        scratch_shapes=[pltpu.VMEM((EXPERT_TILE, D // 2), U32), pltpu.VMEM((2, RUN_TILE_ROWS, D // 2), U32),
                        pltpu.SemaphoreType.DMA, pltpu.SemaphoreType.DMA((2,))],
    )
    return pl.pallas_call(
        _dispatch_body,
        grid_spec=grid_spec,
        out_shape=jax.ShapeDtypeStruct((n_slots, D // 2), U32),
        compiler_params=pltpu.CompilerParams(dimension_semantics=("arbitrary",), vmem_limit_bytes=VMEM_LIMIT,
                                             has_side_effects=True),
        name="dispatch",
    )(tile_off, slot_off, run_len, ends, padded, used, h2, pos)


def _experts_body(layer, te_ref, nact_ref, next_ref, xs_ref, wup_hbm, bup_ref, wdn_hbm, bdn_ref, ys_ref,
                  wup_f, wdn_f, wup_s, wdn_s, sem):
    i = pl.program_id(0)
    ff = wdn_s.shape[0]
    e = te_ref[i]

    def fetch(expert):
        row = layer * N_EXPERTS + expert
        return (pltpu.make_async_copy(wup_hbm.at[row], wup_f, sem.at[0]),
                pltpu.make_async_copy(wdn_hbm.at[row], wdn_f, sem.at[1]))

    @pl.when(i == 0)
    def _():
        for cp in fetch(e):
            cp.start()

    @pl.when((i == 0) | (e != te_ref[jnp.maximum(i - 1, 0)]))
    def _():
        for cp in fetch(e):
            cp.wait()
        n = 2 * LANES
        src = lax.broadcasted_iota(I32, (n, n), 0)
        dst = lax.broadcasted_iota(I32, (n, n), 1)
        perm = (src == jnp.where(dst < LANES, 2 * dst, 2 * (dst - LANES) + 1)).astype(BF16)
        for blk in range(wup_f.shape[1] // n):
            cols = slice(blk * n, (blk + 1) * n)
            wup_s[:, cols] = jnp.dot(wup_f[:, cols].astype(BF16), perm, preferred_element_type=F32).astype(BF16)
        wdn_s[...] = wdn_f[...].astype(BF16)

        @pl.when(next_ref[e] >= 0)
        def _():
            for cp in fetch(next_ref[e]):
                cp.start()

    @pl.when(i < nact_ref[0])
    def _():
        wd = xs_ref[...]
        x = jnp.concatenate([_unpack_hi(wd), _unpack_lo(wd)], axis=1).astype(BF16)
        hh = jnp.dot(x, wup_s[...], preferred_element_type=F32) + bup_ref[0]
        acts = []
        for blk in range(ff // LANES):
            glu = jnp.minimum(hh[:, 2 * blk * LANES:(2 * blk + 1) * LANES], SWIGLU_LIMIT)
            lin = jnp.clip(hh[:, (2 * blk + 1) * LANES:(2 * blk + 2) * LANES], -SWIGLU_LIMIT, SWIGLU_LIMIT)
            acts.append((glu * _sigmoid(SWIGLU_ALPHA * glu) * (lin + 1.0)).astype(BF16))
        act = jnp.concatenate(acts, axis=1)
        y = jnp.dot(act, wdn_s[...], preferred_element_type=F32) + bdn_ref[0]
        ys_ref[...] = _pack_bf16_pairs(y)

    @pl.when(i >= nact_ref[0])
    def _():
        ys_ref[...] = jnp.zeros_like(ys_ref)


def _experts(xs, tile_expert, n_active, next_expert, w_up, b_up, w_dn, b_dn, layer):
    P, W = xs.shape
    tm = EXPERT_TILE
    D = 2 * W
    ff2 = w_up.shape[2]
    ff = w_dn.shape[1]
    ex = lambda i, te, na, nx: (layer * N_EXPERTS + te[i], 0, 0)
    grid_spec = pltpu.PrefetchScalarGridSpec(
        num_scalar_prefetch=3,
        grid=(P // tm,),
        in_specs=[
            pl.BlockSpec((tm, W), lambda i, te, na, nx: (jnp.where(i < na[0], i, 0), 0)),
            pl.BlockSpec(memory_space=pl.ANY),
            pl.BlockSpec((1, 1, ff2), ex),
            pl.BlockSpec(memory_space=pl.ANY),
            pl.BlockSpec((1, 1, D), ex),
        ],
        out_specs=pl.BlockSpec((tm, W), lambda i, te, na, nx: (i, 0)),
        scratch_shapes=[pltpu.VMEM((D, ff2), F32), pltpu.VMEM((ff, D), F32),
                        pltpu.VMEM((D, ff2), BF16), pltpu.VMEM((ff, D), BF16), pltpu.SemaphoreType.DMA((2,))],
    )
    return pl.pallas_call(
        functools.partial(_experts_body, layer),
        grid_spec=grid_spec,
        out_shape=jax.ShapeDtypeStruct((P, W), U32),
        compiler_params=_params(("arbitrary",)),
        name="experts",
    )(tile_expert, n_active, next_expert, xs, w_up, b_up, w_dn, b_dn)


def _combine_body(alpha, tile_off_ref, slot_off_ref, len_ref, used_ref, x_ref, ys_ref, pos_ref, wts_ref, gm_ref,
                  lng_ref, lnb_ref, o_ref, ybuf, sem):
    tt = x_ref.shape[1]
    n = ybuf.shape[1]
    step = pl.program_id(0) * pl.num_programs(1) + pl.program_id(1)
    n_steps = pl.num_programs(0) * pl.num_programs(1)

    def runs(s, action):
        buf = s % 2

        def make_copy(tile_row, slot_row, size):
            return pltpu.make_async_copy(ys_ref.at[pl.ds(slot_row, size)], ybuf.at[buf, pl.ds(tile_row, size)],
                                         sem.at[buf])
        _each_run(s, len_ref, tile_off_ref, slot_off_ref, make_copy, action)

    def fetch(s):
        runs(s, lambda cp: cp.start())
        used = used_ref[s]
        _run_pieces(n - used, lambda off, size: pltpu.make_async_copy(
            ys_ref.at[pl.ds(off, size)], ybuf.at[s % 2, pl.ds(pl.multiple_of(used + off, RUN_ALIGN), size)],
            sem.at[s % 2]).start())

    @pl.when(step == 0)
    def _():
        fetch(step)

    @pl.when(step + 1 < n_steps)
    def _():
        fetch(step + 1)

    pltpu.make_async_copy(ys_ref.at[pl.ds(0, n)], ybuf.at[step % 2], sem.at[step % 2]).wait()

    p_id = lax.broadcasted_iota(I32, (tt, n), 1)
    a = jnp.zeros((tt, n), F32)
    for k in reversed(range(TOP_K)):
        a = jnp.where(p_id == pos_ref[:, k:k + 1], wts_ref[:, k:k + 1], a)
    a = a.astype(BF16)
    wd = ybuf[step % 2]
    hi = jnp.dot(a, _unpack_hi(wd).astype(BF16), preferred_element_type=F32)
    lo = jnp.dot(a, _unpack_lo(wd).astype(BF16), preferred_element_type=F32)
    y = jnp.concatenate([hi, lo], axis=1)
    o_ref[0] = _layer_norm(alpha * x_ref[0] + (1.0 + gm_ref[0]) * y, lng_ref[0], lnb_ref[0])


def _combine(x1, ys, tile_off, slot_off, run_len, used, pos_t, wts_t, gm, ln_g, ln_b, layer, depth):
    B, S, D = x1.shape
    tt = TOKEN_TILE
    nt = S // tt
    grid_spec = pltpu.PrefetchScalarGridSpec(
        num_scalar_prefetch=4,
        grid=(B, nt),
        in_specs=[
            pl.BlockSpec((1, tt, D), lambda b, i, *_: (b, i, 0)),
            pl.BlockSpec(memory_space=pl.ANY),
            pl.BlockSpec((tt, TOP_K), lambda b, i, *_: (b * nt + i, 0)),
            pl.BlockSpec((tt, TOP_K), lambda b, i, *_: (b * nt + i, 0)),
            pl.BlockSpec((1, 1, D), lambda b, i, *_: (b, 0, 0)),
            pl.BlockSpec((1, 1, D), lambda b, i, *_: (layer, 0, 0)),
            pl.BlockSpec((1, 1, D), lambda b, i, *_: (layer, 0, 0)),
        ],
        out_specs=pl.BlockSpec((1, tt, D), lambda b, i, *_: (b, i, 0)),
        scratch_shapes=[pltpu.VMEM((2, RUN_TILE_ROWS, D // 2), U32), pltpu.SemaphoreType.DMA((2,))],
    )
    return pl.pallas_call(
        functools.partial(_combine_body, (2.0 * depth) ** 0.25),
        grid_spec=grid_spec,
        out_shape=jax.ShapeDtypeStruct((B, S, D), F32),
        compiler_params=_params(("arbitrary", "arbitrary")),
        name="combine",
    )(tile_off, slot_off, run_len, used, x1, ys, pos_t, wts_t, gm, ln_g, ln_b)


def _rope_tables(S):
    half = ROPE_DIM // 2
    inv = ROPE_THETA ** (-jnp.arange(half, dtype=F32) / half)
    ang = jnp.arange(S, dtype=F32)[:, None] * inv[None, :]
    cos, sin = jnp.cos(ang), jnp.sin(ang)
    lane = jnp.arange(LANES) % HEAD_DIM
    pick = lane % half
    cos_l, sin_l = cos[:, pick], sin[:, pick]
    cos_t = jnp.where(lane < ROPE_DIM, cos_l, 1.0)
    s_up = jnp.where((lane >= half) & (lane < ROPE_DIM), sin_l, 0.0)
    s_dn = jnp.where(lane < half, -sin_l, 0.0)
    return cos_t.astype(F32), s_up.astype(F32), s_dn.astype(F32)


def kernel(x, c, w_ada, b_ada, w_in, spatial_w, spatial_b, w_branch_a, w_branch_b, w_out, ln1_g, ln1_b,
           w_router, b_router, w_up, b_up, w_down, b_down, ln2_g, ln2_b):
    B, S, D = x.shape
    L = w_ada.shape[0]
    T = B * S
    gw = B_GROUP_WIDTH
    nq = len(DILATIONS) * gw
    assert S % ATT_BLOCK == 0 and S % TOKEN_TILE == 0 and D % (2 * LANES) == 0

    q0, k0, v0, g0 = 2 * D, 2 * D + nq, 2 * D + 2 * nq, 2 * D + 3 * nq
    w_a = jnp.concatenate([w_in[:, :, :2 * D], w_in[:, :, g0:g0 + D]], axis=2).astype(BF16)
    cols = []
    for g in range(len(DILATIONS)):
        for base in (q0, k0, v0):
            cols.append(w_in[:, :, base + g * gw:base + (g + 1) * gw])
    cols.append(w_in[:, :, g0 + D:g0 + 2 * D])
    w_c = jnp.concatenate(cols, axis=2).astype(BF16)
    w_ba = w_branch_a.astype(BF16)
    w_bb = w_branch_b.astype(BF16)
    w_o = w_out.astype(BF16)
    sp_b = spatial_b[..., None]
    w_rt = jnp.swapaxes(w_router, 1, 2)
    b_r = b_router[..., None]
    ff = w_down.shape[2]
    w_up_p = w_up.reshape(L * N_EXPERTS, D, 2 * ff)
    b_up_p = jnp.swapaxes(b_up.reshape(L * N_EXPERTS, 1, ff // LANES, LANES, 2), 3, 4).reshape(
        L * N_EXPERTS, 1, 2 * ff)
    w_dn_p = w_down.reshape(L * N_EXPERTS, ff, D)
    b_dn_p = b_down.reshape(L * N_EXPERTS, 1, D)
    ln1g, ln1b = ln1_g[:, None, :], ln1_b[:, None, :]
    ln2g, ln2b = ln2_g[:, None, :], ln2_b[:, None, :]
    cos_t, sup_t, sdn_t = _rope_tables(S)

    mod = _ada_mod(c, w_ada, b_ada)
    n_slots = T * TOP_K + (T // TOKEN_TILE) * N_EXPERTS * RUN_ALIGN + N_EXPERTS * EXPERT_TILE
    n_slots += EXPERT_TILE
    n_tiles = n_slots // EXPERT_TILE

    for l in range(L):
        sh1, sc1, g1, sh2, sc2, g2 = [mod[l, :, None, i * D:(i + 1) * D] for i in range(6)]
        ya = _branch_a(x, sc1, sh1, w_a, spatial_w, sp_b, w_ba, l)
        *qkv, gb = _qkv(x, sc1, sh1, w_c, cos_t, sup_t, sdn_t, l)
        ob = _attention(qkv)
        x1, h2, pos, wts, cnt = _post_mixer(x, ya, gb, ob, w_bb, w_o, g1, ln1g, ln1b, sc2, sh2, w_rt, b_r, l, L)
        n_tok_tiles = T // TOKEN_TILE
        tc = cnt.reshape(n_tok_tiles, N_EXPERTS, LANES)[:, :, 0].astype(I32)
        tc = ((tc + RUN_ALIGN - 1) // RUN_ALIGN) * RUN_ALIGN
        off_in_tile = jnp.cumsum(tc, axis=1) - tc
        seen_before = jnp.cumsum(tc, axis=0) - tc
        counts = jnp.sum(tc, axis=0)
        padded = ((counts + EXPERT_TILE - 1) // EXPERT_TILE) * EXPERT_TILE
        ends = jnp.cumsum(padded).astype(I32)
        starts = ends - padded
        tile_off = off_in_tile.astype(I32).reshape(-1)
        slot_off = (starts[None, :] + seen_before).astype(I32).reshape(-1)
        run_len = tc.reshape(-1)
        n_active = (ends[-1] // EXPERT_TILE).astype(I32)
        tile_start = jnp.arange(n_tiles, dtype=I32) * EXPERT_TILE
        te = jnp.sum((ends[None, :] <= tile_start[:, None]).astype(I32), axis=1)
        te_last = jnp.sum((ends <= (n_active - 1) * EXPERT_TILE).astype(I32))
        te = jnp.minimum(jnp.where(tile_start < ends[-1], te, te_last), N_EXPERTS - 1).astype(I32)

        padded = padded.astype(I32)
        used = jnp.sum(tc, axis=1).astype(I32)
        xs = _dispatch(h2, pos, tile_off, slot_off, run_len, ends, padded, used, n_slots)
        eid = jnp.arange(N_EXPERTS, dtype=I32)
        later = (eid[None, :] > eid[:, None]) & (padded[None, :] > 0)
        next_expert = jnp.min(jnp.where(later, eid[None, :], N_EXPERTS), axis=1)
        next_expert = jnp.where(next_expert < N_EXPERTS, next_expert, -1).astype(I32)
        ys = _experts(xs, te, n_active.reshape(1), next_expert, w_up_p, b_up_p, w_dn_p, b_dn_p, l)
        x = _combine(x1, ys, tile_off, slot_off, run_len, used, pos.T, wts.T, g2, ln2g, ln2b, l, L)
    return x
```

```python
import functools
import math

import jax
import jax.numpy as jnp
from jax import lax
from jax.experimental import pallas as pl
from jax.experimental.pallas import tpu as pltpu

BF16 = jnp.bfloat16
F32 = jnp.float32
U32 = jnp.uint32
I32 = jnp.int32

A_CHUNK = 128
A_GROUPS = 4
HEAD_DIM = 64
B_HEADS = 8
DILATIONS = (1, 4, 16)
ATT_SPAN = 128
B_GROUP_WIDTH = B_HEADS * HEAD_DIM
ROPE_DIM = HEAD_DIM // 4
ROPE_THETA = 500000.0
N_EXPERTS = 32
TOP_K = 4
SWIGLU_LIMIT = 7.0
SWIGLU_ALPHA = 1.702
LN_EPS = 1e-5

LANES = 128
VMEM_LIMIT = 56 * 1024 * 1024
TOKEN_TILE = 512
MIXER_TILE = 1024
ATT_BLOCK = ATT_SPAN * DILATIONS[-1]
EXPERT_TILE = 512
ATT_CHUNK = 4
RUN_BITS = TOKEN_TILE.bit_length()
RUN_ALIGN = 8
RUN_TILE_ROWS = TOP_K * TOKEN_TILE + N_EXPERTS * RUN_ALIGN
PERM_ROWS = 256


def _gelu(x):
    return 0.5 * x * (1.0 + jnp.tanh(math.sqrt(2.0 / math.pi) * (x + 0.044715 * (x * x * x))))


def _sigmoid(x):
    return 1.0 / (1.0 + jnp.exp(-x))


def _layer_norm(r, g, b):
    mu = jnp.mean(r, axis=-1, keepdims=True)
    c = r - mu
    var = jnp.mean(c * c, axis=-1, keepdims=True)
    return c * lax.rsqrt(var + LN_EPS) * g + b


def _pack_bf16_pairs(x):
    w = x.shape[1] // 2
    bits = lax.bitcast_convert_type(x.astype(BF16).astype(F32), U32)
    return bits[:, :w] | lax.shift_right_logical(bits[:, w:], jnp.uint32(16))


def _unpack_hi(wd):
    return lax.bitcast_convert_type(wd & jnp.uint32(0xFFFF0000), F32)


def _unpack_lo(wd):
    return lax.bitcast_convert_type(lax.shift_left(wd, jnp.uint32(16)), F32)


def _params(sem):
    return pltpu.CompilerParams(dimension_semantics=sem, vmem_limit_bytes=VMEM_LIMIT)


def _ada_body(c_ref, w_ref, b_ref, o_ref):
    o_ref[0] = jnp.dot(c_ref[...], w_ref[0], preferred_element_type=F32,
                       precision=lax.Precision.HIGHEST) + b_ref[0]


def _ada_mod(c, w_ada, b_ada):
    L, D, N = w_ada.shape
    B = c.shape[0]
    tn = D
    return pl.pallas_call(
        _ada_body,
        grid=(L, N // tn),
        in_specs=[
            pl.BlockSpec((B, D), lambda l, j: (0, 0)),
            pl.BlockSpec((1, D, tn), lambda l, j: (l, 0, j)),
            pl.BlockSpec((1, 1, tn), lambda l, j: (l, 0, j)),
        ],
        out_specs=pl.BlockSpec((1, B, tn), lambda l, j: (l, 0, j)),
        out_shape=jax.ShapeDtypeStruct((L, B, N), F32),
        compiler_params=_params(("arbitrary", "arbitrary")),
        name="ada_mod",
    )(c, w_ada, b_ada.reshape(L, 1, N))


def _branch_a_body(x_ref, sc_ref, sh_ref, wa_ref, spw_ref, spb_ref, wba_ref, o_ref, a_scr):
    tm, D = x_ref.shape[1], x_ref.shape[2]
    gd = D // A_GROUPS
    h = (x_ref[0] * (1.0 + sc_ref[0]) + sh_ref[0]).astype(BF16)
    row = lax.broadcasted_iota(I32, (A_CHUNK, A_CHUNK), 0)
    col = lax.broadcasted_iota(I32, (A_CHUNK, A_CHUNK), 1)
    causal = row >= col
    gate = _sigmoid(jnp.dot(h, wa_ref[0, :, 2 * D:3 * D], preferred_element_type=F32))
    for g in range(A_GROUPS):
        v = _gelu(jnp.dot(h, wa_ref[0, :, D + g * gd:D + (g + 1) * gd], preferred_element_type=F32))
        mu = jnp.mean(v, axis=-1, keepdims=True)
        vc = v - mu
        var = jnp.mean(vc * vc, axis=-1, keepdims=True)
        vn = (vc * lax.rsqrt(var + LN_EPS)).astype(BF16)
        u = _gelu(jnp.dot(h, wa_ref[0, :, g * gd:(g + 1) * gd], preferred_element_type=F32))
        ws = jnp.where(causal, spw_ref[0, g], 0.0).astype(BF16)
        bias = spb_ref[0, g]
        for c in range(tm // A_CHUNK):
            rows = slice(c * A_CHUNK, (c + 1) * A_CHUNK)
            sv = jnp.dot(ws, vn[rows], preferred_element_type=F32) + bias
            a_scr[rows, g * gd:(g + 1) * gd] = (u[rows] * sv).astype(BF16)
    ya = jnp.dot(a_scr[...], wba_ref[0], preferred_element_type=F32)
    o_ref[0] = (gate * ya).astype(BF16)


def _branch_a(x, sc, sh, w_a, sp_w, sp_b, w_ba, layer):
    B, S, D = x.shape
    tm = MIXER_TILE
    return pl.pallas_call(
        _branch_a_body,
        grid=(B, S // tm),
        in_specs=[
            pl.BlockSpec((1, tm, D), lambda b, i: (b, i, 0)),
            pl.BlockSpec((1, 1, D), lambda b, i: (b, 0, 0)),
            pl.BlockSpec((1, 1, D), lambda b, i: (b, 0, 0)),
            pl.BlockSpec((1, D, 3 * D), lambda b, i: (layer, 0, 0)),
            pl.BlockSpec((1, A_GROUPS, A_CHUNK, A_CHUNK), lambda b, i: (layer, 0, 0, 0)),
            pl.BlockSpec((1, A_GROUPS, A_CHUNK, 1), lambda b, i: (layer, 0, 0, 0)),
            pl.BlockSpec((1, D, D), lambda b, i: (layer, 0, 0)),
        ],
        out_specs=pl.BlockSpec((1, tm, D), lambda b, i: (b, i, 0)),
        out_shape=jax.ShapeDtypeStruct((B, S, D), BF16),
        scratch_shapes=[pltpu.VMEM((tm, D), BF16)],
        compiler_params=_params(("arbitrary", "arbitrary")),
        name="branch_a",
    )(x, sc, sh, w_a, sp_w, sp_b, w_ba)


def _rope(z, cos, s_up, s_dn):
    half = ROPE_DIM // 2
    parts = []
    for c in range(z.shape[1] // LANES):
        zc = z[:, c * LANES:(c + 1) * LANES]
        parts.append(zc * cos + pltpu.roll(zc, half, 1) * s_up + pltpu.roll(zc, LANES - half, 1) * s_dn)
    return jnp.concatenate(parts, axis=1)


def _qkv_body(x_ref, sc_ref, sh_ref, w_ref, cos_ref, sup_ref, sdn_ref, *rest):
    outs = rest[:9]
    gate_ref = rest[9]
    scr = rest[10]
    tm, D = x_ref.shape[1], x_ref.shape[2]
    gw = B_GROUP_WIDTH
    h = (x_ref[0] * (1.0 + sc_ref[0]) + sh_ref[0]).astype(BF16)
    cos, s_up, s_dn = cos_ref[...], sup_ref[...], sdn_ref[...]
    for g, d in enumerate(DILATIONS):
        for part in range(3):
            col = (3 * g + part) * gw
            z = jnp.dot(h, w_ref[0, :, col:col + gw], preferred_element_type=F32)
            if part < 2:
                z = _rope(z, cos, s_up, s_dn)
            if part == 0:
                z = z * (HEAD_DIM ** -0.5)
            o_ref = outs[3 * g + part]
            if d == 1:
                o_ref[0, 0] = z.astype(BF16)
            else:
                nc = gw // LANES
                for c in range(nc):
                    scr[c] = z[:, c * LANES:(c + 1) * LANES]
                for r in range(d):
                    o_ref[0, r] = jnp.concatenate(
                        [scr[c, pl.ds(r, tm // d, stride=d), :] for c in range(nc)], axis=1).astype(BF16)
    gate = _sigmoid(jnp.dot(h, w_ref[0, :, 9 * gw:9 * gw + D], preferred_element_type=F32))
    gate_ref[0] = gate.astype(BF16)


def _qkv(x, sc, sh, w_c, cos_t, sup_t, sdn_t, layer):
    B, S, D = x.shape
    tm = MIXER_TILE
    gw = B_GROUP_WIDTH
    nw = w_c.shape[2]
    out_shapes, out_specs = [], []
    for d in DILATIONS:
        for _ in range(3):
            out_shapes.append(jax.ShapeDtypeStruct((B, d, S // d, gw), BF16))
            out_specs.append(pl.BlockSpec((1, d, tm // d, gw), lambda b, i: (b, 0, i, 0)))
    out_shapes.append(jax.ShapeDtypeStruct((B, S, D), BF16))
    out_specs.append(pl.BlockSpec((1, tm, D), lambda b, i: (b, i, 0)))
    return pl.pallas_call(
        _qkv_body,
        grid=(B, S // tm),
        in_specs=[
            pl.BlockSpec((1, tm, D), lambda b, i: (b, i, 0)),
            pl.BlockSpec((1, 1, D), lambda b, i: (b, 0, 0)),
            pl.BlockSpec((1, 1, D), lambda b, i: (b, 0, 0)),
            pl.BlockSpec((1, D, nw), lambda b, i: (layer, 0, 0), pipeline_mode=pl.Buffered(1)),
            pl.BlockSpec((tm, LANES), lambda b, i: (i, 0)),
            pl.BlockSpec((tm, LANES), lambda b, i: (i, 0)),
            pl.BlockSpec((tm, LANES), lambda b, i: (i, 0)),
        ],
        out_specs=out_specs,
        out_shape=out_shapes,
        scratch_shapes=[pltpu.VMEM((gw // LANES, tm, LANES), F32)],
        compiler_params=_params(("arbitrary", "arbitrary")),
        name="qkv",
    )(x, sc, sh, w_c, cos_t, sup_t, sdn_t)


def _attention_body(*refs):
    ins = refs[:15]
    o_ref = refs[15]
    o_scr, m_scr, l_scr, bias_scr, p_scr = refs[16:21]
    n = pl.program_id(1)
    blk = ATT_SPAN
    qi = lax.broadcasted_iota(I32, (blk, 2 * blk), 0)
    kj = lax.broadcasted_iota(I32, (blk, 2 * blk), 1)
    band = (kj >= qi) & (kj <= qi + ATT_SPAN)
    neg = jnp.float32(-jnp.inf)
    bias_scr[0] = jnp.where(band, 0.0, neg)
    bias_scr[1] = jnp.where(band & ((kj >= blk) | (n > 0)), 0.0, neg)
    lane = lax.broadcasted_iota(I32, (blk, LANES), 1)
    head0 = lane < HEAD_DIM
    ones = jnp.ones((2 * blk, LANES), BF16)

    def store(scr, g, d, r, j, val):
        view = scr.at[g, pl.ds(j * blk * d, blk * d)]
        if d == 1:
            view[...] = val
        else:
            view[pl.ds(r, blk, stride=d), :] = val

    def window(prev_ref, cur_ref, r, j):
        if j == 0:
            return jnp.concatenate([prev_ref[0, r], cur_ref[0, r, 0:blk]], axis=0)
        return cur_ref[0, r, (j - 1) * blk:(j + 1) * blk]

    chunk_id = 0
    for g, d in enumerate(DILATIONS):
        q_ref, kc_ref, kp_ref, vc_ref, vp_ref = ins[5 * g:5 * g + 5]
        nj = ATT_BLOCK // d // blk
        units = [(r, j) for r in range(d) for j in range(nj)]
        for c0 in range(0, len(units), ATT_CHUNK):
            chunk = units[c0:c0 + ATT_CHUNK]
            pbase = (chunk_id % 2) * 2 * ATT_CHUNK
            chunk_id += 1
            for ui, (r, j) in enumerate(chunk):
                q = q_ref[0, r, j * blk:(j + 1) * blk]
                kk = window(kp_ref, kc_ref, r, j)
                bias = bias_scr[1 if j == 0 else 0]
                zero = jnp.zeros_like(q)
                ms = []
                for h, qh in enumerate((jnp.where(head0, q, zero), jnp.where(head0, zero, q))):
                    s = lax.dot_general(qh, kk, (((1,), (1,)), ((), ())), preferred_element_type=F32) + bias
                    m = jnp.max(s, axis=1, keepdims=True)
                    p_scr[pbase + 2 * ui + h] = jnp.exp(s - m).astype(BF16)
                    ms.append(m)
                store(m_scr, g, d, r, j, jnp.where(head0, ms[0], ms[1]))
            for ui, (r, j) in enumerate(chunk):
                ve = jnp.concatenate([window(vp_ref, vc_ref, r, j), ones], axis=1)
                oe = [jnp.dot(p_scr[pbase + 2 * ui + h], ve, preferred_element_type=F32) for h in range(2)]
                store(o_scr, g, d, r, j, jnp.where(head0, oe[0][:, :LANES], oe[1][:, :LANES]))
                store(l_scr, g, d, r, j, jnp.where(head0, oe[0][:, LANES:], oe[1][:, LANES:]))

    def merge(c, carry):
        rows = pl.ds(pl.multiple_of(c * blk, blk), blk)
        ms = [m_scr[g, rows] for g in range(3)]
        m_ref = jnp.maximum(jnp.maximum(ms[0], ms[1]), ms[2])
        num = jnp.zeros((blk, LANES), F32)
        den = jnp.zeros((blk, LANES), F32)
        for g in range(3):
            w = jnp.exp(ms[g] - m_ref)
            num = num + w * o_scr[g, rows]
            den = den + w * l_scr[g, rows]
        o_ref[0, rows] = (num / den).astype(BF16)
        return carry
    lax.fori_loop(0, ATT_BLOCK // blk, merge, 0)


def _attention(qkv):
    B = qkv[0].shape[0]
    S = qkv[0].shape[2]
    gw = B_GROUP_WIDTH
    blk = ATT_SPAN
    nblk = S // ATT_BLOCK
    in_specs, args = [], []
    for g, d in enumerate(DILATIONS):
        rows = ATT_BLOCK // d
        per = rows // blk
        cur = pl.BlockSpec((1, d, rows, LANES), lambda b, n, hp: (b, 0, n, hp))
        prev = pl.BlockSpec((1, d, blk, LANES),
                            lambda b, n, hp, per=per: (b, 0, jnp.maximum(n * per - 1, 0), hp))
        q, k, v = qkv[3 * g:3 * g + 3]
        in_specs += [cur, cur, prev, cur, prev]
        args += [q, k, k, v, v]
    return pl.pallas_call(
        _attention_body,
        grid=(B, nblk, gw // LANES),
        in_specs=in_specs,
        out_specs=pl.BlockSpec((1, ATT_BLOCK, LANES), lambda b, n, hp: (b, n, hp)),
        out_shape=jax.ShapeDtypeStruct((B, S, gw), BF16),
        scratch_shapes=[pltpu.VMEM((3, ATT_BLOCK, LANES), F32)] * 3 + [
            pltpu.VMEM((2, blk, 2 * blk), F32),
            pltpu.VMEM((4 * ATT_CHUNK, blk, 2 * blk), BF16)],
        compiler_params=_params(("arbitrary", "arbitrary", "arbitrary")),
        name="attention",
    )(*args)


def _post_mixer_body(alpha, x_ref, ya_ref, gb_ref, ob_ref, wbb_ref, wout_ref, gm_ref, lng_ref, lnb_ref,
                     sc_ref, sh_ref, wr_ref, br_ref,
                     x1_ref, h2_ref, pos_ref, wts_ref, cnt_ref):
    tt = x_ref.shape[1]
    yb = jnp.dot(ob_ref[0], wbb_ref[0], preferred_element_type=F32)
    merged = ya_ref[0] + gb_ref[0] * yb.astype(BF16)
    y = jnp.dot(merged, wout_ref[0], preferred_element_type=F32)
    x1 = _layer_norm(alpha * x_ref[0] + (1.0 + gm_ref[0]) * y, lng_ref[0], lnb_ref[0])
    x1_ref[0] = x1
    h2 = x1 * (1.0 + sc_ref[0]) + sh_ref[0]
    h_hi = h2.astype(BF16)
    h2_ref[...] = h_hi

    h_lo = (h2 - h_hi.astype(F32)).astype(BF16)
    w_hi = wr_ref[0].astype(BF16)
    w_lo = (wr_ref[0] - w_hi.astype(F32)).astype(BF16)
    nt_dims = (((1,), (1,)), ((), ()))
    logits = (lax.dot_general(w_hi, h_hi, nt_dims, preferred_element_type=F32)
              + lax.dot_general(w_lo, h_hi, nt_dims, preferred_element_type=F32)
              + lax.dot_general(w_hi, h_lo, nt_dims, preferred_element_type=F32)) + br_ref[0]
    eid = lax.broadcasted_iota(I32, (N_EXPERTS, tt), 0)
    vals, hots = [], []
    cur = logits
    for k in range(TOP_K):
        m = jnp.max(cur, axis=0, keepdims=True)
        sel = jnp.min(jnp.where(cur == m, eid, N_EXPERTS), axis=0, keepdims=True)
        hot = eid == sel
        vals.append(m)
        hots.append(hot)
        cur = jnp.where(hot, -jnp.inf, cur)
    ex = [jnp.exp(v - vals[0]) for v in vals]
    tot = ex[0] + ex[1] + ex[2] + ex[3]
    for k in range(TOP_K):
        wts_ref[k:k + 1, :] = ex[k] / tot

    member = jnp.zeros((N_EXPERTS, tt), F32)
    for hot in hots:
        member = member + hot.astype(F32)
    before = (lax.broadcasted_iota(I32, (tt, tt), 0) < lax.broadcasted_iota(I32, (tt, tt), 1))
    excl = jnp.dot(member.astype(BF16), before.astype(BF16), preferred_element_type=F32)
    cnt = jnp.broadcast_to(jnp.sum(member, axis=1, keepdims=True), (N_EXPERTS, LANES))
    cnt_ref[...] = cnt
    lower = (lax.broadcasted_iota(I32, (N_EXPERTS, N_EXPERTS), 1)
             < lax.broadcasted_iota(I32, (N_EXPERTS, N_EXPERTS), 0)).astype(F32)
    cnt_al = jnp.ceil(cnt * (1.0 / RUN_ALIGN)) * RUN_ALIGN
    off = jnp.dot(lower, cnt_al, preferred_element_type=F32, precision=lax.Precision.HIGHEST)[:, 0:1]
    base = off + excl
    for k in range(TOP_K):
        pos_ref[k:k + 1, :] = jnp.sum(jnp.where(hots[k], base, 0.0), axis=0, keepdims=True).astype(I32)


def _post_mixer(x, ya, gb, ob, w_bb, w_out, gm, ln_g, ln_b, sc, sh, w_rt, b_r, layer, depth):
    B, S, D = x.shape
    tt = TOKEN_TILE
    nt = S // tt
    T = B * S
    gw = B_GROUP_WIDTH
    tok = lambda b, i: (b, i, 0)
    per_b = lambda b, i: (b, 0, 0)
    per_l3 = lambda b, i: (layer, 0, 0)
    flat = lambda b, i: (0, b * nt + i)
    return pl.pallas_call(
        functools.partial(_post_mixer_body, (2.0 * depth) ** 0.25),
        grid=(B, nt),
        in_specs=[
            pl.BlockSpec((1, tt, D), tok),
            pl.BlockSpec((1, tt, D), tok),
            pl.BlockSpec((1, tt, D), tok),
            pl.BlockSpec((1, tt, gw), tok),
            pl.BlockSpec((1, gw, D), per_l3),
            pl.BlockSpec((1, D, D), per_l3),
            pl.BlockSpec((1, 1, D), per_b),
            pl.BlockSpec((1, 1, D), per_l3),
            pl.BlockSpec((1, 1, D), per_l3),
            pl.BlockSpec((1, 1, D), per_b),
            pl.BlockSpec((1, 1, D), per_b),
            pl.BlockSpec((1, N_EXPERTS, D), per_l3),
            pl.BlockSpec((1, N_EXPERTS, 1), per_l3),
        ],
        out_specs=[
            pl.BlockSpec((1, tt, D), tok),
            pl.BlockSpec((tt, D), lambda b, i: (b * nt + i, 0)),
            pl.BlockSpec((TOP_K, tt), flat),
            pl.BlockSpec((TOP_K, tt), flat),
            pl.BlockSpec((N_EXPERTS, LANES), lambda b, i: (b * nt + i, 0)),
        ],
        out_shape=[
            jax.ShapeDtypeStruct((B, S, D), F32),
            jax.ShapeDtypeStruct((T, D), BF16),
            jax.ShapeDtypeStruct((TOP_K, T), I32),
            jax.ShapeDtypeStruct((TOP_K, T), F32),
            jax.ShapeDtypeStruct((B * nt * N_EXPERTS, LANES), F32),
        ],
        compiler_params=_params(("arbitrary", "arbitrary")),
        name="post_mixer",
    )(x, ya, gb, ob, w_bb, w_out, gm, ln_g, ln_b, sc, sh, w_rt, b_r)


def _run_pieces(n, fn):
    for bit in reversed(range(RUN_ALIGN.bit_length() - 1, RUN_BITS)):
        size = 1 << bit
        done = pl.multiple_of((n >> (bit + 1)) << (bit + 1), RUN_ALIGN)

        @pl.when((n & size) != 0)
        def _():
            fn(done, size)


def _each_run(step, len_ref, a_off_ref, b_off_ref, make_copy, action):
    def run(e, carry):
        i = step * N_EXPERTS + e
        a0, b0 = a_off_ref[i], b_off_ref[i]
        _run_pieces(len_ref[i], lambda off, size: action(make_copy(
            pl.multiple_of(a0 + off, RUN_ALIGN), pl.multiple_of(b0 + off, RUN_ALIGN), size)))
        return carry
    lax.fori_loop(0, N_EXPERTS, run, 0)


def _dispatch_body(tile_off_ref, slot_off_ref, len_ref, ends_ref, padded_ref, used_ref, h_ref, pos_ref, xs_ref,
                   zbuf, rbuf, fill_sem, sem):
    t = pl.program_id(0)
    last = pl.num_programs(0) - 1
    tm = zbuf.shape[0]
    tt = h_ref.shape[0]

    @pl.when(t == 0)
    def _():
        zbuf[...] = jnp.zeros_like(zbuf)
        for e in range(N_EXPERTS):
            @pl.when(padded_ref[e] > 0)
            def _():
                start = pl.multiple_of(ends_ref[e] - tm, tm)
                cp = pltpu.make_async_copy(zbuf, xs_ref.at[pl.ds(start, tm)], fill_sem)
                cp.start()
                cp.wait()

        def fill_unused(i, carry):
            cp = pltpu.make_async_copy(zbuf, xs_ref.at[pl.ds(pl.multiple_of(i * tm, tm), tm)], fill_sem)
            cp.start()
            cp.wait()
            return carry
        lax.fori_loop(ends_ref[N_EXPERTS - 1] // tm, xs_ref.shape[0] // tm, fill_unused, 0)

    buf = t % 2
    h = h_ref[...]
    pos = [pos_ref[k:k + 1, :] for k in range(TOP_K)]
    for c in range(rbuf.shape[1] // PERM_ROWS):
        p_id = lax.broadcasted_iota(I32, (PERM_ROWS, tt), 0) + c * PERM_ROWS
        onehot = jnp.zeros((PERM_ROWS, tt), F32)
        for k in range(TOP_K):
            onehot = jnp.where(p_id == pos[k], 1.0, onehot)
        rows = jnp.dot(onehot.astype(BF16), h, preferred_element_type=F32)
        bits = lax.bitcast_convert_type(rows, U32)
        half = bits.shape[1] // 2
        rbuf[buf, c * PERM_ROWS:(c + 1) * PERM_ROWS, :] = (
            bits[:, :half] | lax.shift_right_logical(bits[:, half:], jnp.uint32(16)))

    def runs(s, action):
        def make_copy(tile_row, slot_row, size):
            return pltpu.make_async_copy(rbuf.at[s % 2, pl.ds(tile_row, size)], xs_ref.at[pl.ds(slot_row, size)],
                                         sem.at[s % 2])
        _each_run(s, len_ref, tile_off_ref, slot_off_ref, make_copy, action)

    def wait_step(s):
        pltpu.make_async_copy(rbuf.at[s % 2], xs_ref.at[pl.ds(0, rbuf.shape[1])], sem.at[s % 2]).wait()

    runs(t, lambda cp: cp.start())
    used = used_ref[t]
    spare = xs_ref.shape[0] - tm + buf * (tm // 2)
    _run_pieces(rbuf.shape[1] - used, lambda off, size: pltpu.make_async_copy(
        rbuf.at[buf, pl.ds(pl.multiple_of(used + off, RUN_ALIGN), size)],
        xs_ref.at[pl.ds(pl.multiple_of(spare + off, RUN_ALIGN), size)], sem.at[buf]).start())

    @pl.when(t > 0)
    def _():
        wait_step(t - 1)

    @pl.when(t == last)
    def _():
        wait_step(t)


def _dispatch(h2, pos, tile_off, slot_off, run_len, ends, padded, used, n_slots):
    T, D = h2.shape
    tt = TOKEN_TILE
    grid_spec = pltpu.PrefetchScalarGridSpec(
        num_scalar_prefetch=6,
        grid=(T // tt,),
        in_specs=[pl.BlockSpec((tt, D), lambda t, *_: (t, 0)),
                  pl.BlockSpec((TOP_K, tt), lambda t, *_: (0, t))],
        out_specs=pl.BlockSpec(memory_space=pl.ANY),
        scratch_shapes=[pltpu.VMEM((EXPERT_TILE, D // 2), U32), pltpu.VMEM((2, RUN_TILE_ROWS, D // 2), U32),
                        pltpu.SemaphoreType.DMA, pltpu.SemaphoreType.DMA((2,))],
    )
    return pl.pallas_call(
        _dispatch_body,
        grid_spec=grid_spec,
        out_shape=jax.ShapeDtypeStruct((n_slots, D // 2), U32),
        compiler_params=pltpu.CompilerParams(dimension_semantics=("arbitrary",), vmem_limit_bytes=VMEM_LIMIT,
                                             has_side_effects=True),
        name="dispatch",
    )(tile_off, slot_off, run_len, ends, padded, used, h2, pos)


def _experts_body(layer, te_ref, nact_ref, next_ref, xs_ref, wup_hbm, bup_ref, wdn_hbm, bdn_ref, ys_ref,
                  wup_f, wdn_f, wup_s, wdn_s, sem):
    i = pl.program_id(0)
    ff = wdn_s.shape[0]
    e = te_ref[i]

    def fetch(expert):
        row = layer * N_EXPERTS + expert
        return (pltpu.make_async_copy(wup_hbm.at[row], wup_f, sem.at[0]),
                pltpu.make_async_copy(wdn_hbm.at[row], wdn_f, sem.at[1]))

    @pl.when(i == 0)
    def _():
        for cp in fetch(e):
            cp.start()

    @pl.when((i == 0) | (e != te_ref[jnp.maximum(i - 1, 0)]))
    def _():
        for cp in fetch(e):
            cp.wait()
        n = 2 * LANES
        src = lax.broadcasted_iota(I32, (n, n), 0)
        dst = lax.broadcasted_iota(I32, (n, n), 1)
        perm = (src == jnp.where(dst < LANES, 2 * dst, 2 * (dst - LANES) + 1)).astype(BF16)
        for blk in range(wup_f.shape[1] // n):
            cols = slice(blk * n, (blk + 1) * n)
            wup_s[:, cols] = jnp.dot(wup_f[:, cols].astype(BF16), perm, preferred_element_type=F32).astype(BF16)
        wdn_s[...] = wdn_f[...].astype(BF16)

        @pl.when(next_ref[e] >= 0)
        def _():
            for cp in fetch(next_ref[e]):
                cp.start()

    @pl.when(i < nact_ref[0])
    def _():
        wd = xs_ref[...]
        x = jnp.concatenate([_unpack_hi(wd), _unpack_lo(wd)], axis=1).astype(BF16)
        hh = jnp.dot(x, wup_s[...], preferred_element_type=F32) + bup_ref[0]
        acts = []
        for blk in range(ff // LANES):
            glu = jnp.minimum(hh[:, 2 * blk * LANES:(2 * blk + 1) * LANES], SWIGLU_LIMIT)
            lin = jnp.clip(hh[:, (2 * blk + 1) * LANES:(2 * blk + 2) * LANES], -SWIGLU_LIMIT, SWIGLU_LIMIT)
            acts.append((glu * _sigmoid(SWIGLU_ALPHA * glu) * (lin + 1.0)).astype(BF16))
        act = jnp.concatenate(acts, axis=1)
        y = jnp.dot(act, wdn_s[...], preferred_element_type=F32) + bdn_ref[0]
        ys_ref[...] = _pack_bf16_pairs(y)

    @pl.when(i >= nact_ref[0])
    def _():
        ys_ref[...] = jnp.zeros_like(ys_ref)


def _experts(xs, tile_expert, n_active, next_expert, w_up, b_up, w_dn, b_dn, layer):
    P, W = xs.shape
    tm = EXPERT_TILE
    D = 2 * W
    ff2 = w_up.shape[2]
    ff = w_dn.shape[1]
    ex = lambda i, te, na, nx: (layer * N_EXPERTS + te[i], 0, 0)
    grid_spec = pltpu.PrefetchScalarGridSpec(
        num_scalar_prefetch=3,
        grid=(P // tm,),
        in_specs=[
            pl.BlockSpec((tm, W), lambda i, te, na, nx: (jnp.where(i < na[0], i, 0), 0)),
            pl.BlockSpec(memory_space=pl.ANY),
            pl.BlockSpec((1, 1, ff2), ex),
            pl.BlockSpec(memory_space=pl.ANY),
            pl.BlockSpec((1, 1, D), ex),
        ],
        out_specs=pl.BlockSpec((tm, W), lambda i, te, na, nx: (i, 0)),
        scratch_shapes=[pltpu.VMEM((D, ff2), F32), pltpu.VMEM((ff, D), F32),
                        pltpu.VMEM((D, ff2), BF16), pltpu.VMEM((ff, D), BF16), pltpu.SemaphoreType.DMA((2,))],
    )
    return pl.pallas_call(
        functools.partial(_experts_body, layer),
        grid_spec=grid_spec,
        out_shape=jax.ShapeDtypeStruct((P, W), U32),
        compiler_params=_params(("arbitrary",)),
        name="experts",
    )(tile_expert, n_active, next_expert, xs, w_up, b_up, w_dn, b_dn)


def _combine_body(alpha, tile_off_ref, slot_off_ref, len_ref, used_ref, x_ref, ys_ref, pos_ref, wts_ref, gm_ref,
                  lng_ref, lnb_ref, o_ref, ybuf, sem):
    tt = x_ref.shape[1]
    n = ybuf.shape[1]
    step = pl.program_id(0) * pl.num_programs(1) + pl.program_id(1)
    n_steps = pl.num_programs(0) * pl.num_programs(1)

    def runs(s, action):
        buf = s % 2

        def make_copy(tile_row, slot_row, size):
            return pltpu.make_async_copy(ys_ref.at[pl.ds(slot_row, size)], ybuf.at[buf, pl.ds(tile_row, size)],
                                         sem.at[buf])
        _each_run(s, len_ref, tile_off_ref, slot_off_ref, make_copy, action)

    def fetch(s):
        runs(s, lambda cp: cp.start())
        used = used_ref[s]
        _run_pieces(n - used, lambda off, size: pltpu.make_async_copy(
            ys_ref.at[pl.ds(off, size)], ybuf.at[s % 2, pl.ds(pl.multiple_of(used + off, RUN_ALIGN), size)],
            sem.at[s % 2]).start())

    @pl.when(step == 0)
    def _():
        fetch(step)

    @pl.when(step + 1 < n_steps)
    def _():
        fetch(step + 1)

    pltpu.make_async_copy(ys_ref.at[pl.ds(0, n)], ybuf.at[step % 2], sem.at[step % 2]).wait()

    p_id = lax.broadcasted_iota(I32, (tt, n), 1)
    a = jnp.zeros((tt, n), F32)
    for k in reversed(range(TOP_K)):
        a = jnp.where(p_id == pos_ref[:, k:k + 1], wts_ref[:, k:k + 1], a)
    a = a.astype(BF16)
    wd = ybuf[step % 2]
    hi = jnp.dot(a, _unpack_hi(wd).astype(BF16), preferred_element_type=F32)
    lo = jnp.dot(a, _unpack_lo(wd).astype(BF16), preferred_element_type=F32)
    y = jnp.concatenate([hi, lo], axis=1)
    o_ref[0] = _layer_norm(alpha * x_ref[0] + (1.0 + gm_ref[0]) * y, lng_ref[0], lnb_ref[0])


def _combine(x1, ys, tile_off, slot_off, run_len, used, pos_t, wts_t, gm, ln_g, ln_b, layer, depth):
    B, S, D = x1.shape
    tt = TOKEN_TILE
    nt = S // tt
    grid_spec = pltpu.PrefetchScalarGridSpec(
        num_scalar_prefetch=4,
        grid=(B, nt),
        in_specs=[
            pl.BlockSpec((1, tt, D), lambda b, i, *_: (b, i, 0)),
            pl.BlockSpec(memory_space=pl.ANY),
            pl.BlockSpec((tt, TOP_K), lambda b, i, *_: (b * nt + i, 0)),
            pl.BlockSpec((tt, TOP_K), lambda b, i, *_: (b * nt + i, 0)),
            pl.BlockSpec((1, 1, D), lambda b, i, *_: (b, 0, 0)),
            pl.BlockSpec((1, 1, D), lambda b, i, *_: (layer, 0, 0)),
            pl.BlockSpec((1, 1, D), lambda b, i, *_: (layer, 0, 0)),
        ],
        out_specs=pl.BlockSpec((1, tt, D), lambda b, i, *_: (b, i, 0)),
        scratch_shapes=[pltpu.VMEM((2, RUN_TILE_ROWS, D // 2), U32), pltpu.SemaphoreType.DMA((2,))],
    )
    return pl.pallas_call(
        functools.partial(_combine_body, (2.0 * depth) ** 0.25),
        grid_spec=grid_spec,
        out_shape=jax.ShapeDtypeStruct((B, S, D), F32),
        compiler_params=_params(("arbitrary", "arbitrary")),
        name="combine",
    )(tile_off, slot_off, run_len, used, x1, ys, pos_t, wts_t, gm, ln_g, ln_b)


def _rope_tables(S):
    half = ROPE_DIM // 2
    inv = ROPE_THETA ** (-jnp.arange(half, dtype=F32) / half)
    ang = jnp.arange(S, dtype=F32)[:, None] * inv[None, :]
    cos, sin = jnp.cos(ang), jnp.sin(ang)
    lane = jnp.arange(LANES) % HEAD_DIM
    pick = lane % half
    cos_l, sin_l = cos[:, pick], sin[:, pick]
    cos_t = jnp.where(lane < ROPE_DIM, cos_l, 1.0)
    s_up = jnp.where((lane >= half) & (lane < ROPE_DIM), sin_l, 0.0)
    s_dn = jnp.where(lane < half, -sin_l, 0.0)
    return cos_t.astype(F32), s_up.astype(F32), s_dn.astype(F32)


def kernel(x, c, w_ada, b_ada, w_in, spatial_w, spatial_b, w_branch_a, w_branch_b, w_out, ln1_g, ln1_b,
           w_router, b_router, w_up, b_up, w_down, b_down, ln2_g, ln2_b):
    B, S, D = x.shape
    L = w_ada.shape[0]
    T = B * S
    gw = B_GROUP_WIDTH
    nq = len(DILATIONS) * gw
    assert S % ATT_BLOCK == 0 and S % TOKEN_TILE == 0 and D % (2 * LANES) == 0

    q0, k0, v0, g0 = 2 * D, 2 * D + nq, 2 * D + 2 * nq, 2 * D + 3 * nq
    w_a = jnp.concatenate([w_in[:, :, :2 * D], w_in[:, :, g0:g0 + D]], axis=2).astype(BF16)
    cols = []
    for g in range(len(DILATIONS)):
        for base in (q0, k0, v0):
            cols.append(w_in[:, :, base + g * gw:base + (g + 1) * gw])
    cols.append(w_in[:, :, g0 + D:g0 + 2 * D])
    w_c = jnp.concatenate(cols, axis=2).astype(BF16)
    w_ba = w_branch_a.astype(BF16)
    w_bb = w_branch_b.astype(BF16)
    w_o = w_out.astype(BF16)
    sp_b = spatial_b[..., None]
    w_rt = jnp.swapaxes(w_router, 1, 2)
    b_r = b_router[..., None]
    ff = w_down.shape[2]
    w_up_p = w_up.reshape(L * N_EXPERTS, D, 2 * ff)
    b_up_p = jnp.swapaxes(b_up.reshape(L * N_EXPERTS, 1, ff // LANES, LANES, 2), 3, 4).reshape(
        L * N_EXPERTS, 1, 2 * ff)
    w_dn_p = w_down.reshape(L * N_EXPERTS, ff, D)
    b_dn_p = b_down.reshape(L * N_EXPERTS, 1, D)
    ln1g, ln1b = ln1_g[:, None, :], ln1_b[:, None, :]
    ln2g, ln2b = ln2_g[:, None, :], ln2_b[:, None, :]
    cos_t, sup_t, sdn_t = _rope_tables(S)

    mod = _ada_mod(c, w_ada, b_ada)
    n_slots = T * TOP_K + (T // TOKEN_TILE) * N_EXPERTS * RUN_ALIGN + N_EXPERTS * EXPERT_TILE
    n_slots += EXPERT_TILE
    n_tiles = n_slots // EXPERT_TILE

    for l in range(L):
        sh1, sc1, g1, sh2, sc2, g2 = [mod[l, :, None, i * D:(i + 1) * D] for i in range(6)]
        ya = _branch_a(x, sc1, sh1, w_a, spatial_w, sp_b, w_ba, l)
        *qkv, gb = _qkv(x, sc1, sh1, w_c, cos_t, sup_t, sdn_t, l)
        ob = _attention(qkv)
        x1, h2, pos, wts, cnt = _post_mixer(x, ya, gb, ob, w_bb, w_o, g1, ln1g, ln1b, sc2, sh2, w_rt, b_r, l, L)
        n_tok_tiles = T // TOKEN_TILE
        tc = cnt.reshape(n_tok_tiles, N_EXPERTS, LANES)[:, :, 0].astype(I32)
        tc = ((tc + RUN_ALIGN - 1) // RUN_ALIGN) * RUN_ALIGN
        off_in_tile = jnp.cumsum(tc, axis=1) - tc
        seen_before = jnp.cumsum(tc, axis=0) - tc
        counts = jnp.sum(tc, axis=0)
        padded = ((counts + EXPERT_TILE - 1) // EXPERT_TILE) * EXPERT_TILE
        ends = jnp.cumsum(padded).astype(I32)
        starts = ends - padded
        tile_off = off_in_tile.astype(I32).reshape(-1)
        slot_off = (starts[None, :] + seen_before).astype(I32).reshape(-1)
        run_len = tc.reshape(-1)
        n_active = (ends[-1] // EXPERT_TILE).astype(I32)
        tile_start = jnp.arange(n_tiles, dtype=I32) * EXPERT_TILE
        te = jnp.sum((ends[None, :] <= tile_start[:, None]).astype(I32), axis=1)
        te_last = jnp.sum((ends <= (n_active - 1) * EXPERT_TILE).astype(I32))
        te = jnp.minimum(jnp.where(tile_start < ends[-1], te, te_last), N_EXPERTS - 1).astype(I32)

        padded = padded.astype(I32)
        used = jnp.sum(tc, axis=1).astype(I32)
        xs = _dispatch(h2, pos, tile_off, slot_off, run_len, ends, padded, used, n_slots)
        eid = jnp.arange(N_EXPERTS, dtype=I32)
        later = (eid[None, :] > eid[:, None]) & (padded[None, :] > 0)
        next_expert = jnp.min(jnp.where(later, eid[None, :], N_EXPERTS), axis=1)
        next_expert = jnp.where(next_expert < N_EXPERTS, next_expert, -1).astype(I32)
        ys = _experts(xs, te, n_active.reshape(1), next_expert, w_up_p, b_up_p, w_dn_p, b_dn_p, l)
        x = _combine(x1, ys, tile_off, slot_off, run_len, used, pos.T, wts.T, g2, ln2g, ln2b, l, L)
    return x
```
